```python
import jax, jax.numpy as jnp
from jax import lax
import numpy as np

D_MODEL = 1024
BATCH = 4
SEQ = 4096
DEPTH = 2

MLSTM_HEADS = 4
MLSTM_QK_DIM = 128
MLSTM_V_DIM = 256
MLSTM_CHUNK = 64
CONV_K = 4
SB_HEADS = 16
SB_HEAD_DIM = D_MODEL // SB_HEADS
Q_BLOCK = 128
N_EXPERTS = 64
TOP_K = 8
N_GROUPS = 8
TOPK_GROUPS = 4
EXPERT_FF = 128
SHARED_FF = 256
ROUTED_SCALE = 2.5
EPS = 1e-6

MLSTM_QK = MLSTM_HEADS * MLSTM_QK_DIM
MLSTM_V = MLSTM_HEADS * MLSTM_V_DIM
SB_WIDTH = SB_HEADS * SB_HEAD_DIM
IN_WIDTH = 2 * MLSTM_QK + 2 * MLSTM_V + 2 * MLSTM_HEADS + 3 * SB_WIDTH + 2 * D_MODEL

kernel_name = 'hybrid_mlstm_stickbreak_moe'


def rms_norm(x, w):
    x32 = x.astype(jnp.float32)
    y = x32 * lax.rsqrt(jnp.mean(x32 * x32, axis=-1, keepdims=True) + EPS)
    return (y * w.astype(jnp.float32)).astype(x.dtype)


def modulate(h, shift, scale):
    return h * (1 + scale[:, None, :]) + shift[:, None, :]


def causal_depthwise_conv(u, w, b):
    k_taps = w.shape[0]
    s = u.shape[1]
    up = jnp.pad(u, ((0, 0), (k_taps - 1, 0), (0, 0)))
    out = b
    for j in range(k_taps):
        out = out + up[:, j:j + s, :] * w[j]
    return out


def _mlstm_chunk_step(carry, xs):
    C, n, m = carry
    q, k, v, ig, lf = xs
    L = q.shape[2]
    b = jnp.cumsum(lf, axis=-1)
    a = b[..., -1]
    causal = jnp.tril(jnp.ones((L, L), dtype=bool))
    dmat = jnp.where(causal, b[..., :, None] - b[..., None, :] + ig[..., None, :], -jnp.inf)
    inter = b + m[..., None]
    m_t = jnp.maximum(inter, jnp.max(dmat, axis=-1))
    w_intra = jnp.exp(dmat - m_t[..., None])
    w_inter = jnp.exp(inter - m_t)
    s = jnp.einsum('bhtd,bhsd->bhts', q, k) * w_intra
    num = jnp.einsum('bhts,bhsv->bhtv', s, v) + w_inter[..., None] * jnp.einsum('bhtd,bhdv->bhtv', q, C)
    den = jnp.sum(s, axis=-1) + w_inter * jnp.einsum('bhtd,bhd->bht', q, n)
    h = num / jnp.maximum(jnp.abs(den), jnp.exp(-m_t))[..., None]
    g = a[..., None] - b + ig
    m_new = jnp.maximum(a + m, jnp.max(g, axis=-1))
    wc = jnp.exp(g - m_new[..., None])
    decay = jnp.exp(a + m - m_new)
    C_new = decay[..., None, None] * C + jnp.einsum('bhsd,bhsv->bhdv', k * wc[..., None], v)
    n_new = decay[..., None] * n + jnp.einsum('bhs,bhsd->bhd', wc, k)
    return (C_new, n_new, m_new), h


def mlstm(q, k, v, i_pre, f_pre):
    bsz, s = q.shape[0], q.shape[1]
    nc = s // MLSTM_CHUNK

    def to_chunks(t):
        t = t.astype(jnp.float32).reshape((bsz, nc, MLSTM_CHUNK) + t.shape[2:])
        return jnp.moveaxis(jnp.moveaxis(t, 3, 2), 1, 0)

    xs = (to_chunks(q) * (MLSTM_QK_DIM ** -0.5), to_chunks(k), to_chunks(v),
          to_chunks(i_pre), to_chunks(jax.nn.log_sigmoid(f_pre.astype(jnp.float32))))
    init = (jnp.zeros((bsz, MLSTM_HEADS, MLSTM_QK_DIM, MLSTM_V_DIM), jnp.float32),
            jnp.zeros((bsz, MLSTM_HEADS, MLSTM_QK_DIM), jnp.float32),
            jnp.zeros((bsz, MLSTM_HEADS), jnp.float32))
    _, h = lax.scan(_mlstm_chunk_step, init, xs)
    h = jnp.moveaxis(jnp.moveaxis(h, 0, 1), 2, 3)
    return h.reshape(bsz, s, MLSTM_HEADS, MLSTM_V_DIM)


def stick_breaking_attention(q, k, v):
    s_len, dh = q.shape[2], q.shape[3]
    scale = dh ** -0.5
    outs = []
    for blk in range(s_len // Q_BLOCK):
        t0, t1 = blk * Q_BLOCK, (blk + 1) * Q_BLOCK
        qb = q[:, :, t0:t1].astype(jnp.float32)
        kp = k[:, :, :t1].astype(jnp.float32)
        vp = v[:, :, :t1].astype(jnp.float32)
        z = jnp.einsum('bhtd,bhsd->bhts', qb, kp) * scale
        mask = jnp.arange(t1)[None, :] < jnp.arange(t0, t1)[:, None]
        log_fail = jnp.where(mask, jax.nn.log_sigmoid(-z), 0.0)
        later = lax.cumsum(log_fail, axis=3, reverse=True) - log_fail
        att = jnp.where(mask, jnp.exp(jax.nn.log_sigmoid(z) + later), 0.0)
        outs.append(jnp.einsum('bhts,bhsd->bhtd', att, vp))
    return jnp.concatenate(outs, axis=2).astype(q.dtype)


def hybrid_mixer(h, w_in, conv_w, conv_b, m_igate_b, m_fgate_b, m_norm_w, w_proj_a, w_proj_b, w_out):
    bsz, s, _ = h.shape
    p = h @ w_in
    sizes = [2 * MLSTM_QK, MLSTM_V, MLSTM_V, MLSTM_HEADS, MLSTM_HEADS, SB_WIDTH, SB_WIDTH, SB_WIDTH, D_MODEL, D_MODEL]
    offs, acc = [], 0
    for sz in sizes[:-1]:
        acc += sz
        offs.append(acc)
    mqk, mv, mo, mi, mf, sq, sk, sv, ga, gb = jnp.split(p, offs, axis=-1)
    mqk = jax.nn.silu(causal_depthwise_conv(mqk, conv_w, conv_b))
    mq, mk = jnp.split(mqk, 2, axis=-1)
    hm = mlstm(mq.reshape(bsz, s, MLSTM_HEADS, MLSTM_QK_DIM),
               mk.reshape(bsz, s, MLSTM_HEADS, MLSTM_QK_DIM),
               mv.reshape(bsz, s, MLSTM_HEADS, MLSTM_V_DIM),
               mi + m_igate_b, mf + m_fgate_b)
    hm = hm * lax.rsqrt(jnp.mean(hm * hm, axis=-1, keepdims=True) + EPS)
    hm = (hm.reshape(bsz, s, MLSTM_V) * m_norm_w.astype(jnp.float32)).astype(h.dtype) * jax.nn.sigmoid(mo)
    to_heads = lambda t: t.reshape(bsz, s, SB_HEADS, SB_HEAD_DIM).transpose(0, 2, 1, 3)
    hs = stick_breaking_attention(to_heads(sq), to_heads(sk), to_heads(sv))
    hs = hs.transpose(0, 2, 1, 3).reshape(bsz, s, SB_WIDTH)
    y = jax.nn.sigmoid(ga) * (hm @ w_proj_a) + jax.nn.sigmoid(gb) * (hs @ w_proj_b)
    return y @ w_out


def moe_ffn(u, router_w, router_b, w_gate, w_up, w_down, sh_gate, sh_up, sh_down):
    bsz, s, d = u.shape
    t = u.reshape(-1, d)
    scores = jax.nn.sigmoid((t @ router_w).astype(jnp.float32))
    sel = scores + router_b.astype(jnp.float32)
    grp = sel.reshape(-1, N_GROUPS, N_EXPERTS // N_GROUPS)
    grp_score = jnp.sum(lax.top_k(grp, 2)[0], axis=-1)
    _, gidx = lax.top_k(grp_score, TOPK_GROUPS)
    gmask = jnp.sum(jax.nn.one_hot(gidx, N_GROUPS, dtype=jnp.float32), axis=-2)
    emask = jnp.repeat(gmask, N_EXPERTS // N_GROUPS, axis=-1) > 0
    _, eidx = lax.top_k(jnp.where(emask, sel, -jnp.inf), TOP_K)
    wts = jnp.take_along_axis(scores, eidx, axis=-1)
    wts = wts / jnp.sum(wts, axis=-1, keepdims=True) * ROUTED_SCALE
    gates = jnp.sum(jax.nn.one_hot(eidx, N_EXPERTS, dtype=jnp.float32) * wts[..., None], axis=-2).astype(u.dtype)
    hid = jax.nn.silu(jnp.einsum('nd,edf->nef', t, w_gate)) * jnp.einsum('nd,edf->nef', t, w_up)
    routed = jnp.einsum('nef,efd->nd', hid * gates[..., None], w_down)
    shared = (jax.nn.silu(t @ sh_gate) * (t @ sh_up)) @ sh_down
    return (routed + shared).reshape(bsz, s, d)


def setup_inputs(seed: int = 0) -> dict:
    key = jax.random.key(seed)
    ks = jax.random.split(key, 24)
    nrm = lambda k, shape, sc: jax.random.normal(k, shape, jnp.float32) * sc
    E, F, FS = N_EXPERTS, EXPERT_FF, SHARED_FF
    return {
        'x': nrm(ks[0], (BATCH, SEQ, D_MODEL), 1.0),
        'c': nrm(ks[1], (BATCH, D_MODEL), 1.0),
        'ada_w': nrm(ks[2], (DEPTH, D_MODEL, 6 * D_MODEL), 0.5 * D_MODEL ** -0.5),
        'ada_b': nrm(ks[3], (DEPTH, 6 * D_MODEL), 0.02),
        'norm1_w': 1.0 + nrm(ks[4], (DEPTH, D_MODEL), 0.05),
        'w_in': nrm(ks[5], (DEPTH, D_MODEL, IN_WIDTH), D_MODEL ** -0.5),
        'conv_w': nrm(ks[6], (DEPTH, CONV_K, 2 * MLSTM_QK), CONV_K ** -0.5),
        'conv_b': nrm(ks[7], (DEPTH, 2 * MLSTM_QK), 0.02),
        'm_igate_b': nrm(ks[8], (DEPTH, MLSTM_HEADS), 0.1),
        'm_fgate_b': 3.0 + nrm(ks[9], (DEPTH, MLSTM_HEADS), 0.5),
        'm_norm_w': 1.0 + nrm(ks[10], (DEPTH, MLSTM_V), 0.05),
        'w_proj_a': nrm(ks[11], (DEPTH, MLSTM_V, D_MODEL), MLSTM_V ** -0.5),
        'w_proj_b': nrm(ks[12], (DEPTH, SB_WIDTH, D_MODEL), SB_WIDTH ** -0.5),
        'w_out': nrm(ks[13], (DEPTH, D_MODEL, D_MODEL), D_MODEL ** -0.5),
        'norm2_w': 1.0 + nrm(ks[14], (DEPTH, D_MODEL), 0.05),
        'router_w': nrm(ks[15], (DEPTH, D_MODEL, E), D_MODEL ** -0.5),
        'router_b': nrm(ks[16], (DEPTH, E), 0.01),
        'w_gate': nrm(ks[17], (DEPTH, E, D_MODEL, F), D_MODEL ** -0.5),
        'w_up': nrm(ks[18], (DEPTH, E, D_MODEL, F), D_MODEL ** -0.5),
        'w_down': nrm(ks[19], (DEPTH, E, F, D_MODEL), F ** -0.5),
        'sh_gate': nrm(ks[20], (DEPTH, D_MODEL, FS), D_MODEL ** -0.5),
        'sh_up': nrm(ks[21], (DEPTH, D_MODEL, FS), D_MODEL ** -0.5),
        'sh_down': nrm(ks[22], (DEPTH, FS, D_MODEL), FS ** -0.5),
        'final_norm_w': 1.0 + nrm(ks[23], (D_MODEL,), 0.05),
    }


def reference(x, c, ada_w, ada_b, norm1_w, w_in, conv_w, conv_b, m_igate_b, m_fgate_b, m_norm_w,
              w_proj_a, w_proj_b, w_out, norm2_w, router_w, router_b, w_gate, w_up, w_down,
              sh_gate, sh_up, sh_down, final_norm_w):
    cond = jax.nn.silu(c)
    for l in range(DEPTH):
        mod = cond @ ada_w[l] + ada_b[l]
        sh1, sc1, g1, sh2, sc2, g2 = jnp.split(mod, 6, axis=-1)
        h = modulate(rms_norm(x, norm1_w[l]), sh1, sc1)
        mix = hybrid_mixer(h, w_in[l], conv_w[l], conv_b[l], m_igate_b[l], m_fgate_b[l], m_norm_w[l],
                           w_proj_a[l], w_proj_b[l], w_out[l])
        x = x + (g1[:, None, :] * mix).astype(x.dtype)
        h = modulate(rms_norm(x, norm2_w[l]), sh2, sc2)
        ffn = moe_ffn(h, router_w[l], router_b[l], w_gate[l], w_up[l], w_down[l], sh_gate[l], sh_up[l], sh_down[l])
        x = x + (g2[:, None, :] * ffn).astype(x.dtype)
    return rms_norm(x, final_norm_w)
```

```python
import functools

import jax
import jax.numpy as jnp
from jax import lax
from jax.experimental import pallas as pl
from jax.experimental.pallas import tpu as pltpu

F32 = jnp.float32
BF16 = jnp.bfloat16

D = 1024
DEPTH = 2
N_MHEADS = 4
DK = 128
DV = 256
CONV_K = 4
SB_HEADS = 16
SB_DH = 64
N_EXPERTS = 64
TOP_K = 8
N_GROUPS = 8
TOPK_GROUPS = 4
GROUP_SIZE = N_EXPERTS // N_GROUPS
EXPERT_FF = 128
SHARED_FF = 256
ROUTED_SCALE = 2.5
EPS = 1e-6

LANES = 128
SUBLANES = 8
VMEM_LIMIT = 56 * 1024 * 1024

P_MQK, P_MV, P_MO, P_SQ, P_SK, P_SV, P_GA, P_GB = range(8)

TM_IN = 1024
TN_IN = 1024
L_CHUNK = 256
TQ = 128
TM_MIX = 512
TM_MOE = 1024
R_MOE = 256
NEG_INF = float("-inf")


def _sigmoid(x):
    return 1.0 / (1.0 + jnp.exp(-x))


def _softplus(x):
    return jnp.maximum(x, 0.0) + jnp.log(1.0 + jnp.exp(-jnp.abs(x)))


def _params(sem):
    return pltpu.CompilerParams(dimension_semantics=sem, vmem_limit_bytes=VMEM_LIMIT)


def _ada_kernel(c_ref, w_ref, b_ref, o_ref):
    c = c_ref[...]
    cond = c * _sigmoid(c)
    o_ref[0] = jnp.dot(cond, w_ref[0], preferred_element_type=F32,
                       precision=lax.Precision.HIGHEST) + b_ref[0]


def _ada(c_pad, ada_w, ada_b):
    tn = 1536
    depth = ada_w.shape[0]
    return pl.pallas_call(
        _ada_kernel,
        grid=(depth, 6 * D // tn),
        in_specs=[pl.BlockSpec((SUBLANES, D), lambda l, j: (0, 0)),
                  pl.BlockSpec((1, D, tn), lambda l, j: (l, 0, j)),
                  pl.BlockSpec((1, 1, tn), lambda l, j: (l, 0, j))],
        out_specs=pl.BlockSpec((1, SUBLANES, tn), lambda l, j: (l, 0, j)),
        out_shape=jax.ShapeDtypeStruct((depth, SUBLANES, 6 * D), F32),
        compiler_params=_params(("parallel", "parallel")),
        name="ada_mod",
    )(c_pad, ada_w, ada_b.reshape(depth, 1, 6 * D))


def _inproj_kernel(x_ref, mod_ref, nw_ref, w_ref, wgc_ref, wgr_ref,
                   p_ref, gcol_ref, grow_ref, h_scr):
    @pl.when(pl.program_id(1) == 0)
    def _():
        x = x_ref[...]
        ms = jnp.mean(x * x, axis=-1, keepdims=True)
        y = x * lax.rsqrt(ms + EPS) * nw_ref[...]
        m = mod_ref[0]
        hb = (y * (1.0 + m[1:2]) + m[0:1]).astype(BF16)
        h_scr[...] = hb
        gcol_ref[...] = jnp.dot(hb, wgc_ref[...], preferred_element_type=F32)
        grow_ref[...] = lax.dot_general(wgr_ref[...], hb, (((1,), (1,)), ((), ())),
                                        preferred_element_type=F32)

    p_ref[...] = jnp.dot(h_scr[...], w_ref[...], preferred_element_type=F32).astype(BF16)


def _inproj(x2d, mod_l, norm_w, w_main, wg_col, wg_row, seq):
    n = x2d.shape[0]
    tiles_per_seq = seq // TM_IN
    return pl.pallas_call(
        _inproj_kernel,
        grid=(n // TM_IN, w_main.shape[1] // TN_IN),
        in_specs=[pl.BlockSpec((TM_IN, D), lambda i, j: (i, 0)),
                  pl.BlockSpec((1, 6, D), lambda i, j: (i // tiles_per_seq, 0, 0)),
                  pl.BlockSpec((1, D), lambda i, j: (0, 0)),
                  pl.BlockSpec((D, TN_IN), lambda i, j: (0, j)),
                  pl.BlockSpec((D, LANES), lambda i, j: (0, 0)),
                  pl.BlockSpec((SUBLANES, D), lambda i, j: (0, 0))],
        out_specs=[pl.BlockSpec((TM_IN, TN_IN), lambda i, j: (i, j)),
                   pl.BlockSpec((TM_IN, LANES), lambda i, j: (i, 0)),
                   pl.BlockSpec((SUBLANES, TM_IN), lambda i, j: (0, i))],
        out_shape=[jax.ShapeDtypeStruct((n, w_main.shape[1]), BF16),
                   jax.ShapeDtypeStruct((n, LANES), F32),
                   jax.ShapeDtypeStruct((SUBLANES, n), F32)],
        scratch_shapes=[pltpu.VMEM((TM_IN, D), BF16)],
        compiler_params=_params(("parallel", "arbitrary")),
        name="in_proj",
    )(x2d, mod_l, norm_w.reshape(1, D), w_main, wg_col, wg_row)


def _mlstm_kernel(mqk_ref, mv_ref, mo_ref, gcol_ref, grow_ref, cw_ref, cb_ref,
                  brow_ref, bcol_ref, nw_ref, o_ref, qk_buf, c_scr, m_scr):
    L = L_CHUNK

    @pl.when(pl.program_id(1) == 0)
    def _():
        qk_buf[0:SUBLANES, :] = jnp.zeros((SUBLANES, D), F32)
        c_scr[...] = jnp.zeros_like(c_scr)
        m_scr[...] = jnp.zeros_like(m_scr)

    qk_buf[SUBLANES:SUBLANES + L, :] = mqk_ref[...].astype(F32)
    conv = cb_ref[...] + jnp.zeros((L, D), F32)
    for j in range(CONV_K):
        off = SUBLANES - (CONV_K - 1) + j
        conv = conv + qk_buf[off:off + L, :] * cw_ref[j:j + 1, :]
    qk_buf[0:SUBLANES, :] = qk_buf[L:L + SUBLANES, :]
    qk = conv * _sigmoid(conv)

    gcol = gcol_ref[...] + brow_ref[...]
    grow = grow_ref[...] + bcol_ref[...]
    lf_col_all = -_softplus(-gcol)
    lf_row_all = -_softplus(-grow)

    r_idx = lax.broadcasted_iota(jnp.int32, (L, L), 0)
    c_idx = lax.broadcasted_iota(jnp.int32, (L, L), 1)
    causal = c_idx <= r_idx
    one_col = jnp.where(lax.broadcasted_iota(jnp.int32, (L, LANES), 1) == 0, 1.0, 0.0).astype(BF16)

    for h in range(N_MHEADS):
        ig_row = grow[h:h + 1, :]
        lf_row = lf_row_all[N_MHEADS + h:N_MHEADS + h + 1, :]
        ig_col = gcol[:, h:h + 1]
        lf_col = lf_col_all[:, N_MHEADS + h:N_MHEADS + h + 1]
        m_prev = m_scr[h:h + 1, 0:1]

        b_row = jnp.sum(jnp.where(c_idx >= r_idx, lf_col, 0.0), axis=0, keepdims=True)
        b_col = jnp.sum(jnp.where(causal, lf_row, 0.0), axis=1, keepdims=True)
        u_row = ig_row - b_row
        u_col = ig_col - b_col
        cm_col = jnp.maximum(m_prev, jnp.max(jnp.where(causal, u_row, NEG_INF), axis=1, keepdims=True))
        cm_last = jnp.maximum(m_prev, jnp.max(u_row, axis=1, keepdims=True))
        a_tot = jnp.sum(lf_row, axis=1, keepdims=True)

        w_intra = jnp.where(causal, jnp.exp(u_row - cm_col), 0.0)
        w_inter = jnp.exp(m_prev - cm_col)
        clamp = jnp.exp(-(b_col + cm_col))
        wc_col = jnp.exp(u_col - cm_last)
        decay = jnp.exp(m_prev - cm_last)

        q_h = (qk[:, h * DK:(h + 1) * DK] * (DK ** -0.5)).astype(BF16)
        k_f = qk[:, N_MHEADS * DK + h * DK:N_MHEADS * DK + (h + 1) * DK]
        k_h = k_f.astype(BF16)
        v_aug = jnp.concatenate([mv_ref[:, h * DV:(h + 1) * DV], one_col], axis=1)

        s = lax.dot_general(q_h, k_h, (((1,), (1,)), ((), ())), preferred_element_type=F32)
        s = (s * w_intra).astype(BF16)
        cn = c_scr[h]
        nd = jnp.dot(s, v_aug, preferred_element_type=F32)
        nd = nd + w_inter * jnp.dot(q_h, cn.astype(BF16), preferred_element_type=F32)
        num = nd[:, :DV]
        den = nd[:, DV:DV + 1]
        hh = num / jnp.maximum(jnp.abs(den), clamp)

        kw_t = (k_f * wc_col).T.astype(BF16)
        c_scr[h] = decay * cn + jnp.dot(kw_t, v_aug, preferred_element_type=F32)
        m_scr[h:h + 1, :] = jnp.broadcast_to(a_tot + cm_last, (1, LANES))

        hn = hh * lax.rsqrt(jnp.mean(hh * hh, axis=-1, keepdims=True) + EPS)
        hn = hn * nw_ref[:, h * DV:(h + 1) * DV]
        og = _sigmoid(mo_ref[:, h * DV:(h + 1) * DV].astype(F32))
        o_ref[:, h * DV:(h + 1) * DV] = (hn * og).astype(BF16)


def _mlstm(p, gcol, grow, conv_w, conv_b, brow, bcol, m_norm_w, bsz, seq):
    n = p.shape[0]
    nc = seq // L_CHUNK
    row = lambda b, c: b * nc + c
    return pl.pallas_call(
        _mlstm_kernel,
        grid=(bsz, nc),
        in_specs=[pl.BlockSpec((L_CHUNK, D), lambda b, c: (row(b, c), P_MQK)),
                  pl.BlockSpec((L_CHUNK, D), lambda b, c: (row(b, c), P_MV)),
                  pl.BlockSpec((L_CHUNK, D), lambda b, c: (row(b, c), P_MO)),
                  pl.BlockSpec((L_CHUNK, LANES), lambda b, c: (row(b, c), 0)),
                  pl.BlockSpec((SUBLANES, L_CHUNK), lambda b, c: (0, row(b, c))),
                  pl.BlockSpec((CONV_K, D), lambda b, c: (0, 0)),
                  pl.BlockSpec((1, D), lambda b, c: (0, 0)),
                  pl.BlockSpec((1, LANES), lambda b, c: (0, 0)),
                  pl.BlockSpec((SUBLANES, 1), lambda b, c: (0, 0)),
                  pl.BlockSpec((1, D), lambda b, c: (0, 0))],
        out_specs=pl.BlockSpec((L_CHUNK, D), lambda b, c: (row(b, c), 0)),
        out_shape=jax.ShapeDtypeStruct((n, D), BF16),
        scratch_shapes=[pltpu.VMEM((L_CHUNK + 2 * SUBLANES, D), F32),
                        pltpu.VMEM((N_MHEADS, DK, DV + LANES), F32),
                        pltpu.VMEM((SUBLANES, LANES), F32)],
        compiler_params=_params(("parallel", "arbitrary")),
        name="mlstm",
    )(p, p, p, gcol, grow, conv_w, conv_b.reshape(1, D), brow, bcol, m_norm_w.reshape(1, D))


def _sb_block(qm, k_blk, v_blk, tri, carry, valid):
    z = lax.dot_general(qm, k_blk, (((1,), (1,)), ((), ())), preferred_element_type=F32)
    sp = _softplus(z)
    if valid is not None:
        sp = jnp.where(valid, sp, 0.0)
    hi = sp.astype(BF16)
    lo = (sp - hi.astype(F32)).astype(BF16)
    cum = jnp.dot(jnp.concatenate([hi, lo], axis=1), tri, preferred_element_type=F32)
    logit = z + cum[:, :TQ] + carry
    att = jnp.exp(logit)
    if valid is not None:
        att = jnp.where(valid, att, 0.0)
    out = jnp.dot(att.astype(BF16), v_blk, preferred_element_type=F32)
    return out, carry + cum[:, TQ:]


def _sb_kernel(q_ref, k_ref, v_ref, o_ref):
    i = pl.program_id(2)
    lane = lax.broadcasted_iota(jnp.int32, (1, LANES), 1)
    r_idx = lax.broadcasted_iota(jnp.int32, (TQ, TQ), 0)
    c_idx = lax.broadcasted_iota(jnp.int32, (TQ, TQ), 1)
    strict = c_idx < r_idx
    rr = lax.broadcasted_iota(jnp.int32, (2 * TQ, 2 * TQ), 0)
    cc = lax.broadcasted_iota(jnp.int32, (2 * TQ, 2 * TQ), 1)
    jj = jnp.where(rr >= TQ, rr - TQ, rr)
    tri = jnp.where((cc >= TQ) | (jj >= cc), -1.0, 0.0).astype(BF16)

    q2 = q_ref[...]
    scale = jnp.asarray(SB_DH ** -0.5, BF16)
    zero = jnp.zeros((TQ, LANES), F32)
    outs = []
    for hd in range(2):
        head_lanes = (lane >= hd * SB_DH) & (lane < (hd + 1) * SB_DH)
        qm = jnp.where(head_lanes, q2 * scale, jnp.zeros_like(q2))
        d0 = pl.multiple_of(i * TQ, TQ)
        acc, carry = _sb_block(qm, k_ref[pl.ds(d0, TQ), :], v_ref[pl.ds(d0, TQ), :], tri, zero, strict)

        def body(j, st):
            acc, carry = st
            k0 = pl.multiple_of((i - 1 - j) * TQ, TQ)
            out, carry = _sb_block(qm, k_ref[pl.ds(k0, TQ), :], v_ref[pl.ds(k0, TQ), :], tri, carry, None)
            return acc + out, carry

        acc, carry = lax.fori_loop(0, i, body, (acc, carry))
        outs.append(acc)
    o_ref[...] = jnp.where(lane < SB_DH, outs[0], outs[1]).astype(BF16)


def _sb_attention(p, bsz, seq):
    n = p.shape[0]
    nq = seq // TQ
    pairs = SB_HEADS * SB_DH // LANES
    per_blk = D // LANES
    return pl.pallas_call(
        _sb_kernel,
        grid=(bsz, pairs, nq),
        in_specs=[pl.BlockSpec((TQ, LANES), lambda b, g, i: (b * nq + i, P_SQ * per_blk + g)),
                  pl.BlockSpec((seq, LANES), lambda b, g, i: (b, P_SK * per_blk + g)),
                  pl.BlockSpec((seq, LANES), lambda b, g, i: (b, P_SV * per_blk + g))],
        out_specs=pl.BlockSpec((TQ, LANES), lambda b, g, i: (b * nq + i, g)),
        out_shape=jax.ShapeDtypeStruct((n, D), BF16),
        compiler_params=_params(("parallel", "parallel", "arbitrary")),
        name="sb_attention",
    )(p, p, p)


def _route(h2, rwt_ref, rb_ref):
    tm = h2.shape[0]
    logits = lax.dot_general(rwt_ref[...], h2, (((1,), (1,)), ((), ())),
                             preferred_element_type=F32, precision=lax.Precision.HIGHEST)
    scores = _sigmoid(logits)
    sel = scores + rb_ref[...]
    sub = lax.broadcasted_iota(jnp.int32, (GROUP_SIZE, tm), 0).astype(F32)
    grp_scores = []
    for g in range(N_GROUPS):
        x = sel[g * GROUP_SIZE:(g + 1) * GROUP_SIZE, :]
        m1 = jnp.max(x, axis=0, keepdims=True)
        first = jnp.min(jnp.where(x == m1, sub, float(GROUP_SIZE)), axis=0, keepdims=True)
        m2 = jnp.max(jnp.where(sub == first, NEG_INF, x), axis=0, keepdims=True)
        grp_scores.append(m1 + m2)
    gs = jnp.concatenate(grp_scores, axis=0)
    g_iota = lax.broadcasted_iota(jnp.int32, (N_GROUPS, tm), 0)
    g_rank = jnp.zeros((N_GROUPS, tm), F32)
    for g in range(N_GROUPS):
        row = gs[g:g + 1, :]
        g_rank = g_rank + jnp.where(row > gs, 1.0, jnp.where(row == gs, jnp.where(g_iota > g, 1.0, 0.0), 0.0))
    g_sel = g_rank < TOPK_GROUPS
    masked = jnp.concatenate(
        [jnp.where(g_sel[g:g + 1, :], sel[g * GROUP_SIZE:(g + 1) * GROUP_SIZE, :], NEG_INF)
         for g in range(N_GROUPS)], axis=0)
    e_iota = lax.broadcasted_iota(jnp.int32, (N_EXPERTS, tm), 0)
    e_rank = jnp.zeros((N_EXPERTS, tm), F32)
    for e in range(N_EXPERTS):
        row = masked[e:e + 1, :]
        e_rank = e_rank + jnp.where(row > masked, 1.0,
                                    jnp.where(row == masked, jnp.where(e_iota > e, 1.0, 0.0), 0.0))
    w_sel = jnp.where(e_rank < TOP_K, scores, 0.0)
    gates_t = w_sel / jnp.sum(w_sel, axis=0, keepdims=True) * ROUTED_SCALE
    gates_t = jnp.concatenate([gates_t, jnp.zeros((LANES - N_EXPERTS, tm), F32)], axis=0)
    return gates_t.T


def _mix_kernel(hm_ref, hs_ref, ga_ref, gb_ref, x_ref, mod_ref, wa_ref, wb_ref, wo_ref,
                nw_ref, rwt_ref, rb_ref, x1_ref, h2_ref, gates_ref):
    ya = jnp.dot(hm_ref[...], wa_ref[...], preferred_element_type=F32)
    yb = jnp.dot(hs_ref[...], wb_ref[...], preferred_element_type=F32)
    y = _sigmoid(ga_ref[...].astype(F32)) * ya + _sigmoid(gb_ref[...].astype(F32)) * yb
    mix = jnp.dot(y.astype(BF16), wo_ref[...], preferred_element_type=F32)
    m = mod_ref[0]
    x1 = x_ref[...] + m[2:3] * mix
    x1_ref[...] = x1
    ms = jnp.mean(x1 * x1, axis=-1, keepdims=True)
    h2 = x1 * lax.rsqrt(ms + EPS) * nw_ref[...]
    h2 = h2 * (1.0 + m[4:5]) + m[3:4]
    h2_ref[...] = h2.astype(BF16)
    gates_ref[...] = _route(h2, rwt_ref, rb_ref)


def _mix(hm, hs, p, x2d, mod_l, wa, wb, wo, norm2_w, rwt, rb, seq):
    n = x2d.shape[0]
    tiles_per_seq = seq // TM_MIX
    tile = lambda i: (i, 0)
    const = lambda i: (0, 0)
    return pl.pallas_call(
        _mix_kernel,
        grid=(n // TM_MIX,),
        in_specs=[pl.BlockSpec((TM_MIX, D), tile),
                  pl.BlockSpec((TM_MIX, D), tile),
                  pl.BlockSpec((TM_MIX, D), lambda i: (i, P_GA)),
                  pl.BlockSpec((TM_MIX, D), lambda i: (i, P_GB)),
                  pl.BlockSpec((TM_MIX, D), tile),
                  pl.BlockSpec((1, 6, D), lambda i: (i // tiles_per_seq, 0, 0)),
                  pl.BlockSpec((D, D), const),
                  pl.BlockSpec((D, D), const),
                  pl.BlockSpec((D, D), const),
                  pl.BlockSpec((1, D), const),
                  pl.BlockSpec((N_EXPERTS, D), const),
                  pl.BlockSpec((N_EXPERTS, 1), const)],
        out_specs=[pl.BlockSpec((TM_MIX, D), tile),
                   pl.BlockSpec((TM_MIX, D), tile),
                   pl.BlockSpec((TM_MIX, LANES), tile)],
        out_shape=[jax.ShapeDtypeStruct((n, D), F32),
                   jax.ShapeDtypeStruct((n, D), BF16),
                   jax.ShapeDtypeStruct((n, LANES), F32)],
        compiler_params=_params(("parallel",)),
        name="mix_out",
    )(hm, hs, p, p, x2d, mod_l, wa, wb, wo, norm2_w.reshape(1, D), rwt, rb)


def _moe_kernel(h2_ref, gates_ref, x1_ref, mod_ref, wg_ref, wu_ref, wd_ref,
                sg_ref, su_ref, sd_ref, fnw_ref, o_ref, wg_b, wu_b, wd_b, *, final_norm):
    e = pl.program_id(1)
    n_rows = TM_MOE // R_MOE

    @pl.when(e == 0)
    def _():
        wg_b[...] = sg_ref[...].astype(BF16)
        wu_b[...] = su_ref[...].astype(BF16)
        wd_b[...] = sd_ref[...].astype(BF16)

        def shared(r, _):
            rows = pl.ds(pl.multiple_of(r * R_MOE, R_MOE), R_MOE)
            xr = h2_ref[rows, :]
            g = jnp.dot(xr, wg_b[...], preferred_element_type=F32)
            u = jnp.dot(xr, wu_b[...], preferred_element_type=F32)
            hid = (g * _sigmoid(g) * u).astype(BF16)
            o_ref[rows, :] = jnp.dot(hid, wd_b[...], preferred_element_type=F32)
            return 0

        lax.fori_loop(0, n_rows, shared, 0)

    wg_b[:, :EXPERT_FF] = wg_ref[0].astype(BF16)
    wg_b[:, EXPERT_FF:] = wg_ref[1].astype(BF16)
    wu_b[:, :EXPERT_FF] = wu_ref[0].astype(BF16)
    wu_b[:, EXPERT_FF:] = wu_ref[1].astype(BF16)
    wd_b[:EXPERT_FF, :] = wd_ref[0].astype(BF16)
    wd_b[EXPERT_FF:, :] = wd_ref[1].astype(BF16)
    r_i = lax.broadcasted_iota(jnp.int32, (LANES, 2 * EXPERT_FF), 0)
    c_i = lax.broadcasted_iota(jnp.int32, (LANES, 2 * EXPERT_FF), 1)
    expand = jnp.where(r_i == 2 * e + jnp.where(c_i >= EXPERT_FF, 1, 0), 1.0, 0.0).astype(BF16)

    def routed(r, _):
        rows = pl.ds(pl.multiple_of(r * R_MOE, R_MOE), R_MOE)
        xr = h2_ref[rows, :]
        g = jnp.dot(xr, wg_b[...], preferred_element_type=F32)
        u = jnp.dot(xr, wu_b[...], preferred_element_type=F32)
        gx = jnp.dot(gates_ref[rows, :].astype(BF16), expand, preferred_element_type=F32)
        hid = (g * _sigmoid(g) * u * gx).astype(BF16)
        o_ref[rows, :] += jnp.dot(hid, wd_b[...], preferred_element_type=F32)
        return 0

    lax.fori_loop(0, n_rows, routed, 0)

    @pl.when(e == pl.num_programs(1) - 1)
    def _():
        x2 = x1_ref[...] + mod_ref[0][5:6] * o_ref[...]
        if final_norm:
            ms = jnp.mean(x2 * x2, axis=-1, keepdims=True)
            x2 = x2 * lax.rsqrt(ms + EPS) * fnw_ref[...]
        o_ref[...] = x2


def _moe(h2, gates, x1, mod_l, w_gate, w_up, w_down, sh_gate, sh_up, sh_down, fnw, seq, final_norm):
    n = h2.shape[0]
    tiles_per_seq = seq // TM_MOE
    tile = lambda i, e: (i, 0)
    const = lambda i, e: (0, 0)
    return pl.pallas_call(
        functools.partial(_moe_kernel, final_norm=final_norm),
        grid=(n // TM_MOE, N_EXPERTS // 2),
        in_specs=[pl.BlockSpec((TM_MOE, D), tile),
                  pl.BlockSpec((TM_MOE, LANES), tile),
                  pl.BlockSpec((TM_MOE, D), tile),
                  pl.BlockSpec((1, 6, D), lambda i, e: (i // tiles_per_seq, 0, 0)),
                  pl.BlockSpec((2, D, EXPERT_FF), lambda i, e: (e, 0, 0)),
                  pl.BlockSpec((2, D, EXPERT_FF), lambda i, e: (e, 0, 0)),
                  pl.BlockSpec((2, EXPERT_FF, D), lambda i, e: (e, 0, 0)),
                  pl.BlockSpec((D, SHARED_FF), const),
                  pl.BlockSpec((D, SHARED_FF), const),
                  pl.BlockSpec((SHARED_FF, D), const),
                  pl.BlockSpec((1, D), const)],
        out_specs=pl.BlockSpec((TM_MOE, D), tile),
        out_shape=jax.ShapeDtypeStruct((n, D), F32),
        scratch_shapes=[pltpu.VMEM((D, 2 * EXPERT_FF), BF16),
                        pltpu.VMEM((D, 2 * EXPERT_FF), BF16),
                        pltpu.VMEM((2 * EXPERT_FF, D), BF16)],
        compiler_params=_params(("parallel", "arbitrary")),
        name="moe",
    )(h2, gates, x1, mod_l, w_gate, w_up, w_down, sh_gate, sh_up, sh_down, fnw.reshape(1, D))


def kernel(x, c, ada_w, ada_b, norm1_w, w_in, conv_w, conv_b, m_igate_b, m_fgate_b, m_norm_w, w_proj_a, w_proj_b, w_out, norm2_w, router_w, router_b, w_gate, w_up, w_down, sh_gate, sh_up, sh_down, final_norm_w):
    bsz, seq, _ = x.shape
    n = bsz * seq
    depth = ada_w.shape[0]
    assert SHARED_FF == 2 * EXPERT_FF and seq % TM_IN == 0 and seq % L_CHUNK == 0 and seq % TQ == 0

    c_pad = jnp.zeros((SUBLANES, D), F32).at[:bsz].set(c)
    mod = _ada(c_pad, ada_w, ada_b)[:, :bsz].reshape(depth, bsz, 6, D)

    gate_lo = 2 * N_MHEADS * DK + 2 * N_MHEADS * DV
    gate_hi = gate_lo + 2 * N_MHEADS
    xc = x.reshape(n, D)
    for l in range(depth):
        w_l = w_in[l]
        w_main = jnp.concatenate([w_l[:, :gate_lo], w_l[:, gate_hi:]], axis=1).astype(BF16)
        wg = w_l[:, gate_lo:gate_hi].astype(BF16)
        wg_col = jnp.zeros((D, LANES), BF16).at[:, :2 * N_MHEADS].set(wg)
        gate_b = jnp.concatenate([m_igate_b[l], m_fgate_b[l]]).astype(F32)
        brow = jnp.zeros((1, LANES), F32).at[0, :2 * N_MHEADS].set(gate_b)

        p, gcol, grow = _inproj(xc, mod[l], norm1_w[l], w_main, wg_col, wg.T, seq)
        hm = _mlstm(p, gcol, grow, conv_w[l], conv_b[l], brow, gate_b.reshape(2 * N_MHEADS, 1),
                    m_norm_w[l], bsz, seq)
        hs = _sb_attention(p, bsz, seq)
        x1, h2, gates = _mix(hm, hs, p, xc, mod[l], w_proj_a[l].astype(BF16), w_proj_b[l].astype(BF16),
                             w_out[l].astype(BF16), norm2_w[l], router_w[l].T,
                             router_b[l].reshape(N_EXPERTS, 1), seq)
        xc = _moe(h2, gates, x1, mod[l], w_gate[l], w_up[l], w_down[l], sh_gate[l], sh_up[l], sh_down[l],
                  final_norm_w, seq, final_norm=(l == depth - 1))
    return xc.reshape(bsz, seq, D)
```

```python
import functools

import jax
import jax.numpy as jnp
from jax import lax
from jax.experimental import pallas as pl
from jax.experimental.pallas import tpu as pltpu

F32 = jnp.float32
BF16 = jnp.bfloat16

D = 1024
DEPTH = 2
N_MHEADS = 4
DK = 128
DV = 256
CONV_K = 4
SB_HEADS = 16
SB_DH = 64
N_EXPERTS = 64
TOP_K = 8
N_GROUPS = 8
TOPK_GROUPS = 4
GROUP_SIZE = N_EXPERTS // N_GROUPS
EXPERT_FF = 128
SHARED_FF = 256
ROUTED_SCALE = 2.5
EPS = 1e-6

LANES = 128
SUBLANES = 8
VMEM_LIMIT = 56 * 1024 * 1024

P_MQK, P_MV, P_MO, P_SQ, P_SK, P_SV, P_GA, P_GB = range(8)

TM_IN = 1024
TN_IN = 1024
L_CHUNK = 256
TQ = 256
KB = 128
WIN = 512
SKIP_LOG = -104.0
TM_MIX = 512
TM_MOE = 1024
R_MOE = 1024
NEG_INF = float("-inf")


def _sigmoid(x):
    return 1.0 / (1.0 + jnp.exp(-x))


def _softplus(x):
    return jnp.maximum(x, 0.0) + jnp.log(1.0 + jnp.exp(-jnp.abs(x)))


def _params(sem):
    return pltpu.CompilerParams(dimension_semantics=sem, vmem_limit_bytes=VMEM_LIMIT)


def _ada_kernel(c_ref, w_ref, b_ref, o_ref):
    c = c_ref[...]
    cond = c * _sigmoid(c)
    o_ref[0] = jnp.dot(cond, w_ref[0], preferred_element_type=F32,
                       precision=lax.Precision.HIGHEST) + b_ref[0]


def _ada(c_pad, ada_w, ada_b):
    tn = 1536
    depth = ada_w.shape[0]
    return pl.pallas_call(
        _ada_kernel,
        grid=(depth, 6 * D // tn),
        in_specs=[pl.BlockSpec((SUBLANES, D), lambda l, j: (0, 0)),
                  pl.BlockSpec((1, D, tn), lambda l, j: (l, 0, j)),
                  pl.BlockSpec((1, 1, tn), lambda l, j: (l, 0, j))],
        out_specs=pl.BlockSpec((1, SUBLANES, tn), lambda l, j: (l, 0, j)),
        out_shape=jax.ShapeDtypeStruct((depth, SUBLANES, 6 * D), F32),
        compiler_params=_params(("parallel", "parallel")),
        name="ada_mod",
    )(c_pad, ada_w, ada_b.reshape(depth, 1, 6 * D))


def _inproj_kernel(x_ref, mod_ref, nw_ref, w_ref, wgc_ref, wgr_ref,
                   p_ref, gcol_ref, grow_ref, h_scr):
    @pl.when(pl.program_id(1) == 0)
    def _():
        x = x_ref[...]
        ms = jnp.mean(x * x, axis=-1, keepdims=True)
        y = x * lax.rsqrt(ms + EPS) * nw_ref[...]
        m = mod_ref[0]
        hb = (y * (1.0 + m[1:2]) + m[0:1]).astype(BF16)
        h_scr[...] = hb
        gcol_ref[...] = jnp.dot(hb, wgc_ref[...], preferred_element_type=F32)
        grow_ref[...] = lax.dot_general(wgr_ref[...], hb, (((1,), (1,)), ((), ())),
                                        preferred_element_type=F32)

    p_ref[...] = jnp.dot(h_scr[...], w_ref[...], preferred_element_type=F32).astype(BF16)


def _inproj(x2d, mod_l, norm_w, w_main, wg_col, wg_row, seq):
    n = x2d.shape[0]
    tiles_per_seq = seq // TM_IN
    return pl.pallas_call(
        _inproj_kernel,
        grid=(n // TM_IN, w_main.shape[1] // TN_IN),
        in_specs=[pl.BlockSpec((TM_IN, D), lambda i, j: (i, 0)),
                  pl.BlockSpec((1, 6, D), lambda i, j: (i // tiles_per_seq, 0, 0)),
                  pl.BlockSpec((1, D), lambda i, j: (0, 0)),
                  pl.BlockSpec((D, TN_IN), lambda i, j: (0, j)),
                  pl.BlockSpec((D, LANES), lambda i, j: (0, 0)),
                  pl.BlockSpec((SUBLANES, D), lambda i, j: (0, 0))],
        out_specs=[pl.BlockSpec((TM_IN, TN_IN), lambda i, j: (i, j)),
                   pl.BlockSpec((TM_IN, LANES), lambda i, j: (i, 0)),
                   pl.BlockSpec((SUBLANES, TM_IN), lambda i, j: (0, i))],
        out_shape=[jax.ShapeDtypeStruct((n, w_main.shape[1]), BF16),
                   jax.ShapeDtypeStruct((n, LANES), F32),
                   jax.ShapeDtypeStruct((SUBLANES, n), F32)],
        scratch_shapes=[pltpu.VMEM((TM_IN, D), BF16)],
        compiler_params=_params(("parallel", "arbitrary")),
        name="in_proj",
    )(x2d, mod_l, norm_w.reshape(1, D), w_main, wg_col, wg_row)


def _mlstm_kernel(mqk_ref, mv_ref, mo_ref, gcol_ref, grow_ref, cw_ref, cb_ref,
                  brow_ref, bcol_ref, nw_ref, o_ref, qk_buf, c_scr, m_scr):
    L = L_CHUNK

    @pl.when(pl.program_id(1) == 0)
    def _():
        qk_buf[0:SUBLANES, :] = jnp.zeros((SUBLANES, D), F32)
        c_scr[...] = jnp.zeros_like(c_scr)
        m_scr[...] = jnp.zeros_like(m_scr)

    qk_buf[SUBLANES:SUBLANES + L, :] = mqk_ref[...].astype(F32)
    conv = cb_ref[...] + jnp.zeros((L, D), F32)
    for j in range(CONV_K):
        off = SUBLANES - (CONV_K - 1) + j
        conv = conv + qk_buf[off:off + L, :] * cw_ref[j:j + 1, :]
    qk_buf[0:SUBLANES, :] = qk_buf[L:L + SUBLANES, :]
    qk = conv * _sigmoid(conv)

    gcol = gcol_ref[...] + brow_ref[...]
    grow = grow_ref[...] + bcol_ref[...]
    lf_col_all = -_softplus(-gcol)
    lf_row_all = -_softplus(-grow)

    r_idx = lax.broadcasted_iota(jnp.int32, (L, L), 0)
    c_idx = lax.broadcasted_iota(jnp.int32, (L, L), 1)
    causal = c_idx <= r_idx
    one_col = jnp.where(lax.broadcasted_iota(jnp.int32, (L, LANES), 1) == 0, 1.0, 0.0).astype(BF16)

    for h in range(N_MHEADS):
        ig_row = grow[h:h + 1, :]
        lf_row = lf_row_all[N_MHEADS + h:N_MHEADS + h + 1, :]
        ig_col = gcol[:, h:h + 1]
        lf_col = lf_col_all[:, N_MHEADS + h:N_MHEADS + h + 1]
        m_prev = m_scr[h:h + 1, 0:1]

        b_row = jnp.sum(jnp.where(c_idx >= r_idx, lf_col, 0.0), axis=0, keepdims=True)
        b_col = jnp.sum(jnp.where(causal, lf_row, 0.0), axis=1, keepdims=True)
        u_row = ig_row - b_row
        u_col = ig_col - b_col
        cm_col = jnp.maximum(m_prev, jnp.max(jnp.where(causal, u_row, NEG_INF), axis=1, keepdims=True))
        cm_last = jnp.maximum(m_prev, jnp.max(u_row, axis=1, keepdims=True))
        a_tot = jnp.sum(lf_row, axis=1, keepdims=True)

        w_intra = jnp.where(causal, jnp.exp(u_row - cm_col), 0.0)
        w_inter = jnp.exp(m_prev - cm_col)
        clamp = jnp.exp(-(b_col + cm_col))
        wc_col = jnp.exp(u_col - cm_last)
        decay = jnp.exp(m_prev - cm_last)

        q_h = (qk[:, h * DK:(h + 1) * DK] * (DK ** -0.5)).astype(BF16)
        k_f = qk[:, N_MHEADS * DK + h * DK:N_MHEADS * DK + (h + 1) * DK]
        k_h = k_f.astype(BF16)
        v_aug = jnp.concatenate([mv_ref[:, h * DV:(h + 1) * DV], one_col], axis=1)

        s = lax.dot_general(q_h, k_h, (((1,), (1,)), ((), ())), preferred_element_type=F32)
        s = (s * w_intra).astype(BF16)
        cn = c_scr[h]
        nd = jnp.dot(s, v_aug, preferred_element_type=F32)
        nd = nd + w_inter * jnp.dot(q_h, cn.astype(BF16), preferred_element_type=F32)
        num = nd[:, :DV]
        den = nd[:, DV:DV + 1]
        hh = num / jnp.maximum(jnp.abs(den), clamp)

        kw_t = (k_f * wc_col).T.astype(BF16)
        c_scr[h] = decay * cn + jnp.dot(kw_t, v_aug, preferred_element_type=F32)
        m_scr[h:h + 1, :] = jnp.broadcast_to(a_tot + cm_last, (1, LANES))

        hn = hh * lax.rsqrt(jnp.mean(hh * hh, axis=-1, keepdims=True) + EPS)
        hn = hn * nw_ref[:, h * DV:(h + 1) * DV]
        og = _sigmoid(mo_ref[:, h * DV:(h + 1) * DV].astype(F32))
        o_ref[:, h * DV:(h + 1) * DV] = (hn * og).astype(BF16)


def _mlstm(p, gcol, grow, conv_w, conv_b, brow, bcol, m_norm_w, bsz, seq):
    n = p.shape[0]
    nc = seq // L_CHUNK
    row = lambda b, c: b * nc + c
    return pl.pallas_call(
        _mlstm_kernel,
        grid=(bsz, nc),
        in_specs=[pl.BlockSpec((L_CHUNK, D), lambda b, c: (row(b, c), P_MQK)),
                  pl.BlockSpec((L_CHUNK, D), lambda b, c: (row(b, c), P_MV)),
                  pl.BlockSpec((L_CHUNK, D), lambda b, c: (row(b, c), P_MO)),
                  pl.BlockSpec((L_CHUNK, LANES), lambda b, c: (row(b, c), 0)),
                  pl.BlockSpec((SUBLANES, L_CHUNK), lambda b, c: (0, row(b, c))),
                  pl.BlockSpec((CONV_K, D), lambda b, c: (0, 0)),
                  pl.BlockSpec((1, D), lambda b, c: (0, 0)),
                  pl.BlockSpec((1, LANES), lambda b, c: (0, 0)),
                  pl.BlockSpec((SUBLANES, 1), lambda b, c: (0, 0)),
                  pl.BlockSpec((1, D), lambda b, c: (0, 0))],
        out_specs=pl.BlockSpec((L_CHUNK, D), lambda b, c: (row(b, c), 0)),
        out_shape=jax.ShapeDtypeStruct((n, D), BF16),
        scratch_shapes=[pltpu.VMEM((L_CHUNK + 2 * SUBLANES, D), F32),
                        pltpu.VMEM((N_MHEADS, DK, DV + LANES), F32),
                        pltpu.VMEM((SUBLANES, LANES), F32)],
        compiler_params=_params(("parallel", "arbitrary")),
        name="mlstm",
    )(p, p, p, gcol, grow, conv_w, conv_b.reshape(1, D), brow, bcol, m_norm_w.reshape(1, D))


def _sb_cum(sp_blocks, tri):
    rows = []
    for sp in sp_blocks:
        hi = sp.astype(BF16)
        lo = (sp - hi.astype(F32)).astype(BF16)
        rows.append(jnp.concatenate([hi, lo], axis=1))
    lhs = rows[0] if len(rows) == 1 else jnp.concatenate(rows, axis=0)
    cum = jnp.dot(lhs, tri, preferred_element_type=F32)
    return [cum[r * TQ:(r + 1) * TQ] for r in range(len(sp_blocks))]


def _nt_dot(a, b):
    return lax.dot_general(a, b, (((1,), (1,)), ((), ())), preferred_element_type=F32)


def _sb_kernel(q_ref, k_ref, v_ref, tri_ref, o_ref):
    i = pl.program_id(2)
    tri = tri_ref[...]
    lane = lax.broadcasted_iota(jnp.int32, (1, LANES), 1)
    q2 = q_ref[...] * jnp.asarray(SB_DH ** -0.5, BF16)
    qms = [jnp.where((lane >= hd * SB_DH) & (lane < (hd + 1) * SB_DH), q2, jnp.zeros_like(q2))
           for hd in range(2)]

    ws = pl.multiple_of(jnp.maximum(i * TQ - (WIN - TQ), 0), TQ)
    delta = i * TQ - ws
    k_win = k_ref[pl.ds(ws, WIN), :]
    v_win = v_ref[pl.ds(ws, WIN), :]
    diff = (lax.broadcasted_iota(jnp.int32, (TQ, WIN), 1)
            - lax.broadcasted_iota(jnp.int32, (TQ, WIN), 0))
    valid = diff < delta
    nb = WIN // KB
    zs = [_nt_dot(qm, k_win) for qm in qms]
    sps = [jnp.where(valid, _softplus(z), 0.0) for z in zs]
    cums = _sb_cum([sp[:, kb * KB:(kb + 1) * KB] for sp in sps for kb in range(nb)], tri)
    accs, carries = [], []
    for hd in range(2):
        carry = jnp.zeros((TQ, KB), F32)
        logits = [None] * nb
        for kb in reversed(range(nb)):
            cum = cums[hd * nb + kb]
            logits[kb] = zs[hd][:, kb * KB:(kb + 1) * KB] + cum[:, :KB] + carry
            carry = carry + cum[:, KB:]
        att = jnp.where(valid, jnp.exp(jnp.concatenate(logits, axis=1)), 0.0).astype(BF16)
        accs.append(jnp.dot(att, v_win, preferred_element_type=F32))
        carries.append(carry)

    n_rem = ws // KB

    def live(c0, c1):
        return (jnp.max(jnp.maximum(c0, c1)) > SKIP_LOG).astype(jnp.int32)

    def cond(st):
        return jnp.logical_and(st[0] < n_rem, st[1] > 0)

    def body(st):
        j, _, a0, a1, c0, c1 = st
        k0 = pl.multiple_of((n_rem - 1 - j) * KB, KB)
        k_blk = k_ref[pl.ds(k0, KB), :]
        v_blk = v_ref[pl.ds(k0, KB), :]
        z_b = [_nt_dot(qm, k_blk) for qm in qms]
        cum_b = _sb_cum([_softplus(z) for z in z_b], tri)
        acc_b, car_b = [a0, a1], [c0, c1]
        for hd in range(2):
            att = jnp.exp(z_b[hd] + cum_b[hd][:, :KB] + car_b[hd]).astype(BF16)
            acc_b[hd] = acc_b[hd] + jnp.dot(att, v_blk, preferred_element_type=F32)
            car_b[hd] = car_b[hd] + cum_b[hd][:, KB:]
        return (j + 1, live(car_b[0], car_b[1]), acc_b[0], acc_b[1], car_b[0], car_b[1])

    st = lax.while_loop(cond, body, (jnp.int32(0), live(carries[0], carries[1]),
                                     accs[0], accs[1], carries[0], carries[1]))
    o_ref[...] = jnp.where(lane < SB_DH, st[2], st[3]).astype(BF16)


def _sb_attention(p, bsz, seq):
    n = p.shape[0]
    nq = seq // TQ
    pairs = SB_HEADS * SB_DH // LANES
    per_blk = D // LANES
    rr = lax.broadcasted_iota(jnp.int32, (2 * KB, 2 * KB), 0) % KB
    cc = lax.broadcasted_iota(jnp.int32, (2 * KB, 2 * KB), 1)
    tri = jnp.where((cc >= KB) | (rr >= cc), -1.0, 0.0).astype(BF16)
    return pl.pallas_call(
        _sb_kernel,
        grid=(bsz, pairs, nq),
        in_specs=[pl.BlockSpec((TQ, LANES), lambda b, g, i: (b * nq + i, P_SQ * per_blk + g)),
                  pl.BlockSpec((seq, LANES), lambda b, g, i: (b, P_SK * per_blk + g)),
                  pl.BlockSpec((seq, LANES), lambda b, g, i: (b, P_SV * per_blk + g)),
                  pl.BlockSpec((2 * KB, 2 * KB), lambda b, g, i: (0, 0))],
        out_specs=pl.BlockSpec((TQ, LANES), lambda b, g, i: (b * nq + i, g)),
        out_shape=jax.ShapeDtypeStruct((n, D), BF16),
        compiler_params=_params(("parallel", "parallel", "arbitrary")),
        name="sb_attention",
    )(p, p, p, tri)


def _route(h2, rwt_ref, rb_ref):
    tm = h2.shape[0]
    logits = lax.dot_general(rwt_ref[...], h2, (((1,), (1,)), ((), ())),
                             preferred_element_type=F32, precision=lax.Precision.HIGHEST)
    scores = _sigmoid(logits)
    sel = scores + rb_ref[...]
    sub = lax.broadcasted_iota(jnp.int32, (GROUP_SIZE, tm), 0).astype(F32)
    grp_scores = []
    for g in range(N_GROUPS):
        x = sel[g * GROUP_SIZE:(g + 1) * GROUP_SIZE, :]
        m1 = jnp.max(x, axis=0, keepdims=True)
        first = jnp.min(jnp.where(x == m1, sub, float(GROUP_SIZE)), axis=0, keepdims=True)
        m2 = jnp.max(jnp.where(sub == first, NEG_INF, x), axis=0, keepdims=True)
        grp_scores.append(m1 + m2)
    gs = jnp.concatenate(grp_scores, axis=0)
    g_iota = lax.broadcasted_iota(jnp.int32, (N_GROUPS, tm), 0)
    g_rank = jnp.zeros((N_GROUPS, tm), F32)
    for g in range(N_GROUPS):
        row = gs[g:g + 1, :]
        g_rank = g_rank + jnp.where(row > gs, 1.0, jnp.where(row == gs, jnp.where(g_iota > g, 1.0, 0.0), 0.0))
    g_sel = g_rank < TOPK_GROUPS
    masked = jnp.concatenate(
        [jnp.where(g_sel[g:g + 1, :], sel[g * GROUP_SIZE:(g + 1) * GROUP_SIZE, :], NEG_INF)
         for g in range(N_GROUPS)], axis=0)
    e_iota = lax.broadcasted_iota(jnp.int32, (N_EXPERTS, tm), 0)
    e_rank = jnp.zeros((N_EXPERTS, tm), F32)
    for e in range(N_EXPERTS):
        row = masked[e:e + 1, :]
        e_rank = e_rank + jnp.where(row > masked, 1.0,
                                    jnp.where(row == masked, jnp.where(e_iota > e, 1.0, 0.0), 0.0))
    w_sel = jnp.where(e_rank < TOP_K, scores, 0.0)
    gates_t = w_sel / jnp.sum(w_sel, axis=0, keepdims=True) * ROUTED_SCALE
    gates_t = jnp.concatenate([gates_t, jnp.zeros((LANES - N_EXPERTS, tm), F32)], axis=0)
    return gates_t.T


def _mix_kernel(hm_ref, hs_ref, ga_ref, gb_ref, x_ref, mod_ref, wa_ref, wb_ref, wo_ref,
                nw_ref, rwt_ref, rb_ref, x1_ref, h2_ref, gates_ref):
    ya = jnp.dot(hm_ref[...], wa_ref[...], preferred_element_type=F32)
    yb = jnp.dot(hs_ref[...], wb_ref[...], preferred_element_type=F32)
    y = _sigmoid(ga_ref[...].astype(F32)) * ya + _sigmoid(gb_ref[...].astype(F32)) * yb
    mix = jnp.dot(y.astype(BF16), wo_ref[...], preferred_element_type=F32)
    m = mod_ref[0]
    x1 = x_ref[...] + m[2:3] * mix
    x1_ref[...] = x1
    ms = jnp.mean(x1 * x1, axis=-1, keepdims=True)
    h2 = x1 * lax.rsqrt(ms + EPS) * nw_ref[...]
    h2 = h2 * (1.0 + m[4:5]) + m[3:4]
    h2_ref[...] = h2.astype(BF16)
    gates_ref[...] = _route(h2, rwt_ref, rb_ref)


def _mix(hm, hs, p, x2d, mod_l, wa, wb, wo, norm2_w, rwt, rb, seq):
    n = x2d.shape[0]
    tiles_per_seq = seq // TM_MIX
    tile = lambda i: (i, 0)
    const = lambda i: (0, 0)
    return pl.pallas_call(
        _mix_kernel,
        grid=(n // TM_MIX,),
        in_specs=[pl.BlockSpec((TM_MIX, D), tile),
                  pl.BlockSpec((TM_MIX, D), tile),
                  pl.BlockSpec((TM_MIX, D), lambda i: (i, P_GA)),
                  pl.BlockSpec((TM_MIX, D), lambda i: (i, P_GB)),
                  pl.BlockSpec((TM_MIX, D), tile),
                  pl.BlockSpec((1, 6, D), lambda i: (i // tiles_per_seq, 0, 0)),
                  pl.BlockSpec((D, D), const),
                  pl.BlockSpec((D, D), const),
                  pl.BlockSpec((D, D), const),
                  pl.BlockSpec((1, D), const),
                  pl.BlockSpec((N_EXPERTS, D), const),
                  pl.BlockSpec((N_EXPERTS, 1), const)],
        out_specs=[pl.BlockSpec((TM_MIX, D), tile),
                   pl.BlockSpec((TM_MIX, D), tile),
                   pl.BlockSpec((TM_MIX, LANES), tile)],
        out_shape=[jax.ShapeDtypeStruct((n, D), F32),
                   jax.ShapeDtypeStruct((n, D), BF16),
                   jax.ShapeDtypeStruct((n, LANES), F32)],
        compiler_params=_params(("parallel",)),
        name="mix_out",
    )(hm, hs, p, p, x2d, mod_l, wa, wb, wo, norm2_w.reshape(1, D), rwt, rb)


def _moe_kernel(h2_ref, gates_ref, x1_ref, mod_ref, wg_ref, wu_ref, wd_ref,
                sg_ref, su_ref, sd_ref, fnw_ref, o_ref, wg_b, wu_b, wd_b, *, final_norm):
    e = pl.program_id(1)
    n_rows = TM_MOE // R_MOE

    @pl.when(e == 0)
    def _():
        wg_b[...] = sg_ref[...].astype(BF16)
        wu_b[...] = su_ref[...].astype(BF16)
        wd_b[...] = sd_ref[...].astype(BF16)

        def shared(r, _):
            rows = pl.ds(pl.multiple_of(r * R_MOE, R_MOE), R_MOE)
            xr = h2_ref[rows, :]
            g = jnp.dot(xr, wg_b[...], preferred_element_type=F32)
            u = jnp.dot(xr, wu_b[...], preferred_element_type=F32)
            hid = (g * _sigmoid(g) * u).astype(BF16)
            o_ref[rows, :] = jnp.dot(hid, wd_b[...], preferred_element_type=F32)
            return 0

        if n_rows == 1:
            shared(0, 0)
        else:
            lax.fori_loop(0, n_rows, shared, 0)

    wg_b[:, :EXPERT_FF] = wg_ref[0].astype(BF16)
    wg_b[:, EXPERT_FF:] = wg_ref[1].astype(BF16)
    wu_b[:, :EXPERT_FF] = wu_ref[0].astype(BF16)
    wu_b[:, EXPERT_FF:] = wu_ref[1].astype(BF16)
    wd_b[:EXPERT_FF, :] = wd_ref[0].astype(BF16)
    wd_b[EXPERT_FF:, :] = wd_ref[1].astype(BF16)
    r_i = lax.broadcasted_iota(jnp.int32, (LANES, 2 * EXPERT_FF), 0)
    c_i = lax.broadcasted_iota(jnp.int32, (LANES, 2 * EXPERT_FF), 1)
    expand = jnp.where(r_i == 2 * e + jnp.where(c_i >= EXPERT_FF, 1, 0), 1.0, 0.0).astype(BF16)

    def routed(r, _):
        rows = pl.ds(pl.multiple_of(r * R_MOE, R_MOE), R_MOE)
        xr = h2_ref[rows, :]
        g = jnp.dot(xr, wg_b[...], preferred_element_type=F32)
        u = jnp.dot(xr, wu_b[...], preferred_element_type=F32)
        gx = jnp.dot(gates_ref[rows, :].astype(BF16), expand, preferred_element_type=F32)
        hid = (g * _sigmoid(g) * u * gx).astype(BF16)
        o_ref[rows, :] += jnp.dot(hid, wd_b[...], preferred_element_type=F32)
        return 0

    if n_rows == 1:
        routed(0, 0)
    else:
        lax.fori_loop(0, n_rows, routed, 0)

    @pl.when(e == pl.num_programs(1) - 1)
    def _():
        x2 = x1_ref[...] + mod_ref[0][5:6] * o_ref[...]
        if final_norm:
            ms = jnp.mean(x2 * x2, axis=-1, keepdims=True)
            x2 = x2 * lax.rsqrt(ms + EPS) * fnw_ref[...]
        o_ref[...] = x2


def _moe(h2, gates, x1, mod_l, w_gate, w_up, w_down, sh_gate, sh_up, sh_down, fnw, seq, final_norm):
    n = h2.shape[0]
    tiles_per_seq = seq // TM_MOE
    tile = lambda i, e: (i, 0)
    const = lambda i, e: (0, 0)
    return pl.pallas_call(
        functools.partial(_moe_kernel, final_norm=final_norm),
        grid=(n // TM_MOE, N_EXPERTS // 2),
        in_specs=[pl.BlockSpec((TM_MOE, D), tile),
                  pl.BlockSpec((TM_MOE, LANES), tile),
                  pl.BlockSpec((TM_MOE, D), tile),
                  pl.BlockSpec((1, 6, D), lambda i, e: (i // tiles_per_seq, 0, 0)),
                  pl.BlockSpec((2, D, EXPERT_FF), lambda i, e: (e, 0, 0)),
                  pl.BlockSpec((2, D, EXPERT_FF), lambda i, e: (e, 0, 0)),
                  pl.BlockSpec((2, EXPERT_FF, D), lambda i, e: (e, 0, 0)),
                  pl.BlockSpec((D, SHARED_FF), const),
                  pl.BlockSpec((D, SHARED_FF), const),
                  pl.BlockSpec((SHARED_FF, D), const),
                  pl.BlockSpec((1, D), const)],
        out_specs=pl.BlockSpec((TM_MOE, D), tile),
        out_shape=jax.ShapeDtypeStruct((n, D), F32),
        scratch_shapes=[pltpu.VMEM((D, 2 * EXPERT_FF), BF16),
                        pltpu.VMEM((D, 2 * EXPERT_FF), BF16),
                        pltpu.VMEM((2 * EXPERT_FF, D), BF16)],
        compiler_params=_params(("parallel", "arbitrary")),
        name="moe",
    )(h2, gates, x1, mod_l, w_gate, w_up, w_down, sh_gate, sh_up, sh_down, fnw.reshape(1, D))


def kernel(x, c, ada_w, ada_b, norm1_w, w_in, conv_w, conv_b, m_igate_b, m_fgate_b, m_norm_w, w_proj_a, w_proj_b, w_out, norm2_w, router_w, router_b, w_gate, w_up, w_down, sh_gate, sh_up, sh_down, final_norm_w):
    bsz, seq, _ = x.shape
    n = bsz * seq
    depth = ada_w.shape[0]
    assert SHARED_FF == 2 * EXPERT_FF and seq % TM_IN == 0 and seq % L_CHUNK == 0 and seq % TQ == 0 and seq >= WIN and WIN % TQ == 0

    c_pad = jnp.zeros((SUBLANES, D), F32).at[:bsz].set(c)
    mod = _ada(c_pad, ada_w, ada_b)[:, :bsz].reshape(depth, bsz, 6, D)

    gate_lo = 2 * N_MHEADS * DK + 2 * N_MHEADS * DV
    gate_hi = gate_lo + 2 * N_MHEADS
    xc = x.reshape(n, D)
    for l in range(depth):
        w_l = w_in[l]
        w_main = jnp.concatenate([w_l[:, :gate_lo], w_l[:, gate_hi:]], axis=1).astype(BF16)
        wg = w_l[:, gate_lo:gate_hi].astype(BF16)
        wg_col = jnp.zeros((D, LANES), BF16).at[:, :2 * N_MHEADS].set(wg)
        gate_b = jnp.concatenate([m_igate_b[l], m_fgate_b[l]]).astype(F32)
        brow = jnp.zeros((1, LANES), F32).at[0, :2 * N_MHEADS].set(gate_b)

        p, gcol, grow = _inproj(xc, mod[l], norm1_w[l], w_main, wg_col, wg.T, seq)
        hm = _mlstm(p, gcol, grow, conv_w[l], conv_b[l], brow, gate_b.reshape(2 * N_MHEADS, 1),
                    m_norm_w[l], bsz, seq)
        hs = _sb_attention(p, bsz, seq)
        x1, h2, gates = _mix(hm, hs, p, xc, mod[l], w_proj_a[l].astype(BF16), w_proj_b[l].astype(BF16),
                             w_out[l].astype(BF16), norm2_w[l], router_w[l].T,
                             router_b[l].reshape(N_EXPERTS, 1), seq)
        xc = _moe(h2, gates, x1, mod[l], w_gate[l], w_up[l], w_down[l], sh_gate[l], sh_up[l], sh_down[l],
                  final_norm_w, seq, final_norm=(l == depth - 1))
    return xc.reshape(bsz, seq, D)
```

```python
import functools

import jax
import jax.numpy as jnp
from jax import lax
from jax.experimental import pallas as pl
from jax.experimental.pallas import tpu as pltpu

F32 = jnp.float32
BF16 = jnp.bfloat16

D = 1024
DEPTH = 2
N_MHEADS = 4
DK = 128
DV = 256
CONV_K = 4
SB_HEADS = 16
SB_DH = 64
N_EXPERTS = 64
TOP_K = 8
N_GROUPS = 8
TOPK_GROUPS = 4
GROUP_SIZE = N_EXPERTS // N_GROUPS
EXPERT_FF = 128
SHARED_FF = 256
ROUTED_SCALE = 2.5
EPS = 1e-6

LANES = 128
SUBLANES = 8
VMEM_LIMIT = 56 * 1024 * 1024

P_MQK, P_MV, P_MO, P_SQ, P_SK, P_SV, P_GA, P_GB = range(8)

TM_IN = 1024
TN_IN = 1024
L_CHUNK = 256
TQ = 256
KB = 128
WIN = 384
SKIP_LOG = -104.0
TM_MIX = 512
TM_MOE = 2048
R_MOE = 1024
NEG_INF = float("-inf")


def _sigmoid(x):
    return 1.0 / (1.0 + jnp.exp(-x))


def _softplus(x):
    return jnp.maximum(x, 0.0) + jnp.log(1.0 + jnp.exp(-jnp.abs(x)))


def _params(sem):
    return pltpu.CompilerParams(dimension_semantics=sem, vmem_limit_bytes=VMEM_LIMIT)


def _ada_kernel(c_ref, w_ref, b_ref, o_ref):
    c = c_ref[...]
    cond = c * _sigmoid(c)
    o_ref[0] = jnp.dot(cond, w_ref[0], preferred_element_type=F32,
                       precision=lax.Precision.HIGHEST) + b_ref[0]


def _ada(c_pad, ada_w, ada_b):
    tn = 1536
    depth = ada_w.shape[0]
    return pl.pallas_call(
        _ada_kernel,
        grid=(depth, 6 * D // tn),
        in_specs=[pl.BlockSpec((SUBLANES, D), lambda l, j: (0, 0)),
                  pl.BlockSpec((1, D, tn), lambda l, j: (l, 0, j)),
                  pl.BlockSpec((1, 1, tn), lambda l, j: (l, 0, j))],
        out_specs=pl.BlockSpec((1, SUBLANES, tn), lambda l, j: (l, 0, j)),
        out_shape=jax.ShapeDtypeStruct((depth, SUBLANES, 6 * D), F32),
        compiler_params=_params(("parallel", "parallel")),
        name="ada_mod",
    )(c_pad, ada_w, ada_b.reshape(depth, 1, 6 * D))


def _inproj_kernel(x_ref, mod_ref, nw_ref, w_ref, wgc_ref, wgr_ref,
                   p_ref, gcol_ref, grow_ref, h_scr):
    @pl.when(pl.program_id(1) == 0)
    def _():
        x = x_ref[...]
        ms = jnp.mean(x * x, axis=-1, keepdims=True)
        y = x * lax.rsqrt(ms + EPS) * nw_ref[...]
        m = mod_ref[0]
        hb = (y * (1.0 + m[1:2]) + m[0:1]).astype(BF16)
        h_scr[...] = hb
        gcol_ref[...] = jnp.dot(hb, wgc_ref[...], preferred_element_type=F32)
        grow_ref[...] = lax.dot_general(wgr_ref[...], hb, (((1,), (1,)), ((), ())),
                                        preferred_element_type=F32)

    p_ref[...] = jnp.dot(h_scr[...], w_ref[...], preferred_element_type=F32).astype(BF16)


def _inproj(x2d, mod_l, norm_w, w_main, wg_col, wg_row, seq):
    n = x2d.shape[0]
    tiles_per_seq = seq // TM_IN
    return pl.pallas_call(
        _inproj_kernel,
        grid=(n // TM_IN, w_main.shape[1] // TN_IN),
        in_specs=[pl.BlockSpec((TM_IN, D), lambda i, j: (i, 0)),
                  pl.BlockSpec((1, 6, D), lambda i, j: (i // tiles_per_seq, 0, 0)),
                  pl.BlockSpec((1, D), lambda i, j: (0, 0)),
                  pl.BlockSpec((D, TN_IN), lambda i, j: (0, j)),
                  pl.BlockSpec((D, LANES), lambda i, j: (0, 0)),
                  pl.BlockSpec((SUBLANES, D), lambda i, j: (0, 0))],
        out_specs=[pl.BlockSpec((TM_IN, TN_IN), lambda i, j: (i, j)),
                   pl.BlockSpec((TM_IN, LANES), lambda i, j: (i, 0)),
                   pl.BlockSpec((SUBLANES, TM_IN), lambda i, j: (0, i))],
        out_shape=[jax.ShapeDtypeStruct((n, w_main.shape[1]), BF16),
                   jax.ShapeDtypeStruct((n, LANES), F32),
                   jax.ShapeDtypeStruct((SUBLANES, n), F32)],
        scratch_shapes=[pltpu.VMEM((TM_IN, D), BF16)],
        compiler_params=_params(("parallel", "arbitrary")),
        name="in_proj",
    )(x2d, mod_l, norm_w.reshape(1, D), w_main, wg_col, wg_row)


def _mlstm_kernel(mqk_ref, mv_ref, mo_ref, gcol_ref, grow_ref, cw_ref, cb_ref,
                  brow_ref, bcol_ref, nw_ref, o_ref, qk_buf, c_scr, m_scr):
    L = L_CHUNK

    @pl.when(pl.program_id(1) == 0)
    def _():
        qk_buf[0:SUBLANES, :] = jnp.zeros((SUBLANES, D), F32)
        c_scr[...] = jnp.zeros_like(c_scr)
        m_scr[...] = jnp.zeros_like(m_scr)

    qk_buf[SUBLANES:SUBLANES + L, :] = mqk_ref[...].astype(F32)
    conv = cb_ref[...] + jnp.zeros((L, D), F32)
    for j in range(CONV_K):
        off = SUBLANES - (CONV_K - 1) + j
        conv = conv + qk_buf[off:off + L, :] * cw_ref[j:j + 1, :]
    qk_buf[0:SUBLANES, :] = qk_buf[L:L + SUBLANES, :]
    qk = conv * _sigmoid(conv)

    gcol = gcol_ref[...] + brow_ref[...]
    grow = grow_ref[...] + bcol_ref[...]
    lf_col_all = -_softplus(-gcol)
    lf_row_all = -_softplus(-grow)

    r_idx = lax.broadcasted_iota(jnp.int32, (L, L), 0)
    c_idx = lax.broadcasted_iota(jnp.int32, (L, L), 1)
    causal = c_idx <= r_idx
    one_col = jnp.where(lax.broadcasted_iota(jnp.int32, (L, LANES), 1) == 0, 1.0, 0.0).astype(BF16)

    for h in range(N_MHEADS):
        ig_row = grow[h:h + 1, :]
        lf_row = lf_row_all[N_MHEADS + h:N_MHEADS + h + 1, :]
        ig_col = gcol[:, h:h + 1]
        lf_col = lf_col_all[:, N_MHEADS + h:N_MHEADS + h + 1]
        m_prev = m_scr[h:h + 1, 0:1]

        b_row = jnp.sum(jnp.where(c_idx >= r_idx, lf_col, 0.0), axis=0, keepdims=True)
        b_col = jnp.sum(jnp.where(causal, lf_row, 0.0), axis=1, keepdims=True)
        u_row = ig_row - b_row
        u_col = ig_col - b_col
        cm_col = jnp.maximum(m_prev, jnp.max(jnp.where(causal, u_row, NEG_INF), axis=1, keepdims=True))
        cm_last = jnp.maximum(m_prev, jnp.max(u_row, axis=1, keepdims=True))
        a_tot = jnp.sum(lf_row, axis=1, keepdims=True)

        w_intra = jnp.where(causal, jnp.exp(u_row - cm_col), 0.0)
        w_inter = jnp.exp(m_prev - cm_col)
        clamp = jnp.exp(-(b_col + cm_col))
        wc_col = jnp.exp(u_col - cm_last)
        decay = jnp.exp(m_prev - cm_last)

        q_h = (qk[:, h * DK:(h + 1) * DK] * (DK ** -0.5)).astype(BF16)
        k_f = qk[:, N_MHEADS * DK + h * DK:N_MHEADS * DK + (h + 1) * DK]
        k_h = k_f.astype(BF16)
        v_aug = jnp.concatenate([mv_ref[:, h * DV:(h + 1) * DV], one_col], axis=1)

        s = lax.dot_general(q_h, k_h, (((1,), (1,)), ((), ())), preferred_element_type=F32)
        s = (s * w_intra).astype(BF16)
        cn = c_scr[h]
        nd = jnp.dot(s, v_aug, preferred_element_type=F32)
        nd = nd + w_inter * jnp.dot(q_h, cn.astype(BF16), preferred_element_type=F32)
        num = nd[:, :DV]
        den = nd[:, DV:DV + 1]
        hh = num / jnp.maximum(jnp.abs(den), clamp)

        kw_t = (k_f * wc_col).T.astype(BF16)
        c_scr[h] = decay * cn + jnp.dot(kw_t, v_aug, preferred_element_type=F32)
        m_scr[h:h + 1, :] = jnp.broadcast_to(a_tot + cm_last, (1, LANES))

        hn = hh * lax.rsqrt(jnp.mean(hh * hh, axis=-1, keepdims=True) + EPS)
        hn = hn * nw_ref[:, h * DV:(h + 1) * DV]
        og = _sigmoid(mo_ref[:, h * DV:(h + 1) * DV].astype(F32))
        o_ref[:, h * DV:(h + 1) * DV] = (hn * og).astype(BF16)


def _mlstm(p, gcol, grow, conv_w, conv_b, brow, bcol, m_norm_w, bsz, seq):
    n = p.shape[0]
    nc = seq // L_CHUNK
    row = lambda b, c: b * nc + c
    return pl.pallas_call(
        _mlstm_kernel,
        grid=(bsz, nc),
        in_specs=[pl.BlockSpec((L_CHUNK, D), lambda b, c: (row(b, c), P_MQK)),
                  pl.BlockSpec((L_CHUNK, D), lambda b, c: (row(b, c), P_MV)),
                  pl.BlockSpec((L_CHUNK, D), lambda b, c: (row(b, c), P_MO)),
                  pl.BlockSpec((L_CHUNK, LANES), lambda b, c: (row(b, c), 0)),
                  pl.BlockSpec((SUBLANES, L_CHUNK), lambda b, c: (0, row(b, c))),
                  pl.BlockSpec((CONV_K, D), lambda b, c: (0, 0)),
                  pl.BlockSpec((1, D), lambda b, c: (0, 0)),
                  pl.BlockSpec((1, LANES), lambda b, c: (0, 0)),
                  pl.BlockSpec((SUBLANES, 1), lambda b, c: (0, 0)),
                  pl.BlockSpec((1, D), lambda b, c: (0, 0))],
        out_specs=pl.BlockSpec((L_CHUNK, D), lambda b, c: (row(b, c), 0)),
        out_shape=jax.ShapeDtypeStruct((n, D), BF16),
        scratch_shapes=[pltpu.VMEM((L_CHUNK + 2 * SUBLANES, D), F32),
                        pltpu.VMEM((N_MHEADS, DK, DV + LANES), F32),
                        pltpu.VMEM((SUBLANES, LANES), F32)],
        compiler_params=_params(("parallel", "arbitrary")),
        name="mlstm",
    )(p, p, p, gcol, grow, conv_w, conv_b.reshape(1, D), brow, bcol, m_norm_w.reshape(1, D))


def _sb_cum(sp_blocks, tri):
    rows = []
    for sp in sp_blocks:
        hi = sp.astype(BF16)
        lo = (sp - hi.astype(F32)).astype(BF16)
        rows.append(jnp.concatenate([hi, lo], axis=1))
    lhs = rows[0] if len(rows) == 1 else jnp.concatenate(rows, axis=0)
    cum = jnp.dot(lhs, tri, preferred_element_type=F32)
    return [cum[r * KB:(r + 1) * KB] for r in range(len(sp_blocks))]


def _nt_dot(a, b):
    return lax.dot_general(a, b, (((1,), (1,)), ((), ())), preferred_element_type=F32)


def _sb_tail(qm2, accs, carries, n_rem, k_ref, v_ref, tri):
    def live(c0, c1):
        return (jnp.max(jnp.maximum(c0, c1)) > SKIP_LOG).astype(jnp.int32)

    def cond(st):
        return jnp.logical_and(st[0] < n_rem, st[1] > 0)

    def body(st):
        j, _, a0, a1, c0, c1 = st
        k0 = pl.multiple_of((n_rem - 1 - j) * KB, KB)
        k_blk = k_ref[pl.ds(k0, KB), :]
        v_blk = v_ref[pl.ds(k0, KB), :]
        z_b = [_nt_dot(qm, k_blk) for qm in qm2]
        cum_b = _sb_cum([_softplus(z) for z in z_b], tri)
        acc_b, car_b = [a0, a1], [c0, c1]
        for hd in range(2):
            att = jnp.exp(z_b[hd] + cum_b[hd][:, :KB] + car_b[hd]).astype(BF16)
            acc_b[hd] = acc_b[hd] + jnp.dot(att, v_blk, preferred_element_type=F32)
            car_b[hd] = car_b[hd] + cum_b[hd][:, KB:]
        return (j + 1, live(car_b[0], car_b[1]), acc_b[0], acc_b[1], car_b[0], car_b[1])

    st = lax.while_loop(cond, body, (jnp.int32(0), live(carries[0], carries[1]),
                                     accs[0], accs[1], carries[0], carries[1]))
    return st[2], st[3]


def _sb_kernel(q_ref, k_ref, v_ref, tri_ref, o_ref):
    i = pl.program_id(2)
    tri = tri_ref[...]
    lane = lax.broadcasted_iota(jnp.int32, (1, LANES), 1)
    q2 = q_ref[...] * jnp.asarray(SB_DH ** -0.5, BF16)
    qms = [jnp.where((lane >= hd * SB_DH) & (lane < (hd + 1) * SB_DH), q2, jnp.zeros_like(q2))
           for hd in range(2)]
    nsub = TQ // KB
    nb = WIN // KB
    diff = (lax.broadcasted_iota(jnp.int32, (KB, WIN), 1)
            - lax.broadcasted_iota(jnp.int32, (KB, WIN), 0))

    zs, sps, valids, v_wins, n_rems = [], [], [], [], []
    for s in range(nsub):
        qs = i * TQ + s * KB
        ws = pl.multiple_of(jnp.maximum(qs - (WIN - KB), 0), KB)
        valid = diff < (qs - ws)
        k_win = k_ref[pl.ds(ws, WIN), :]
        for hd in range(2):
            z = _nt_dot(qms[hd][s * KB:(s + 1) * KB], k_win)
            zs.append(z)
            sps.append(jnp.where(valid, _softplus(z), 0.0))
        valids.append(valid)
        v_wins.append(v_ref[pl.ds(ws, WIN), :])
        n_rems.append(ws // KB)
    cums = _sb_cum([sp[:, kb * KB:(kb + 1) * KB] for sp in sps for kb in range(nb)], tri)

    outs = []
    for s in range(nsub):
        accs, carries = [], []
        for hd in range(2):
            u = s * 2 + hd
            carry = jnp.zeros((KB, KB), F32)
            logits = [None] * nb
            for kb in reversed(range(nb)):
                cum = cums[u * nb + kb]
                logits[kb] = zs[u][:, kb * KB:(kb + 1) * KB] + cum[:, :KB] + carry
                carry = carry + cum[:, KB:]
            att = jnp.where(valids[s], jnp.exp(jnp.concatenate(logits, axis=1)), 0.0).astype(BF16)
            accs.append(jnp.dot(att, v_wins[s], preferred_element_type=F32))
            carries.append(carry)
        qm2 = [qms[hd][s * KB:(s + 1) * KB] for hd in range(2)]
        a0, a1 = _sb_tail(qm2, accs, carries, n_rems[s], k_ref, v_ref, tri)
        outs.append(jnp.where(lane < SB_DH, a0, a1).astype(BF16))
    o_ref[...] = jnp.concatenate(outs, axis=0)


def _sb_attention(p, bsz, seq):
    n = p.shape[0]
    nq = seq // TQ
    pairs = SB_HEADS * SB_DH // LANES
    per_blk = D // LANES
    rr = lax.broadcasted_iota(jnp.int32, (2 * KB, 2 * KB), 0) % KB
    cc = lax.broadcasted_iota(jnp.int32, (2 * KB, 2 * KB), 1)
    tri = jnp.where((cc >= KB) | (rr >= cc), -1.0, 0.0).astype(BF16)
    return pl.pallas_call(
        _sb_kernel,
        grid=(bsz, pairs, nq),
        in_specs=[pl.BlockSpec((TQ, LANES), lambda b, g, i: (b * nq + i, P_SQ * per_blk + g)),
                  pl.BlockSpec((seq, LANES), lambda b, g, i: (b, P_SK * per_blk + g)),
                  pl.BlockSpec((seq, LANES), lambda b, g, i: (b, P_SV * per_blk + g)),
                  pl.BlockSpec((2 * KB, 2 * KB), lambda b, g, i: (0, 0))],
        out_specs=pl.BlockSpec((TQ, LANES), lambda b, g, i: (b * nq + i, g)),
        out_shape=jax.ShapeDtypeStruct((n, D), BF16),
        compiler_params=_params(("parallel", "parallel", "arbitrary")),
        name="sb_attention",
    )(p, p, p, tri)


def _route(h2, rwt_ref, rb_ref):
    tm = h2.shape[0]
    logits = lax.dot_general(rwt_ref[...], h2, (((1,), (1,)), ((), ())),
                             preferred_element_type=F32, precision=lax.Precision.HIGHEST)
    scores = _sigmoid(logits)
    sel = scores + rb_ref[...]
    sub = lax.broadcasted_iota(jnp.int32, (GROUP_SIZE, tm), 0).astype(F32)
    grp_scores = []
    for g in range(N_GROUPS):
        x = sel[g * GROUP_SIZE:(g + 1) * GROUP_SIZE, :]
        m1 = jnp.max(x, axis=0, keepdims=True)
        first = jnp.min(jnp.where(x == m1, sub, float(GROUP_SIZE)), axis=0, keepdims=True)
        m2 = jnp.max(jnp.where(sub == first, NEG_INF, x), axis=0, keepdims=True)
        grp_scores.append(m1 + m2)
    gs = jnp.concatenate(grp_scores, axis=0)
    g_iota = lax.broadcasted_iota(jnp.int32, (N_GROUPS, tm), 0)
    g_rank = jnp.zeros((N_GROUPS, tm), F32)
    for g in range(N_GROUPS):
        row = gs[g:g + 1, :]
        g_rank = g_rank + jnp.where(row > gs, 1.0, jnp.where(row == gs, jnp.where(g_iota > g, 1.0, 0.0), 0.0))
    g_sel = g_rank < TOPK_GROUPS
    masked = jnp.concatenate(
        [jnp.where(g_sel[g:g + 1, :], sel[g * GROUP_SIZE:(g + 1) * GROUP_SIZE, :], NEG_INF)
         for g in range(N_GROUPS)], axis=0)
    e_iota = lax.broadcasted_iota(jnp.int32, (N_EXPERTS, tm), 0)
    e_rank = jnp.zeros((N_EXPERTS, tm), F32)
    for e in range(N_EXPERTS):
        row = masked[e:e + 1, :]
        e_rank = e_rank + jnp.where(row > masked, 1.0,
                                    jnp.where(row == masked, jnp.where(e_iota > e, 1.0, 0.0), 0.0))
    w_sel = jnp.where(e_rank < TOP_K, scores, 0.0)
    gates_t = w_sel / jnp.sum(w_sel, axis=0, keepdims=True) * ROUTED_SCALE
    gates_t = jnp.concatenate([gates_t, jnp.zeros((LANES - N_EXPERTS, tm), F32)], axis=0)
    return gates_t.T


def _mix_kernel(hm_ref, hs_ref, ga_ref, gb_ref, x_ref, mod_ref, wa_ref, wb_ref, wo_ref,
                nw_ref, rwt_ref, rb_ref, x1_ref, h2_ref, gates_ref):
    ya = jnp.dot(hm_ref[...], wa_ref[...], preferred_element_type=F32)
    yb = jnp.dot(hs_ref[...], wb_ref[...], preferred_element_type=F32)
    y = _sigmoid(ga_ref[...].astype(F32)) * ya + _sigmoid(gb_ref[...].astype(F32)) * yb
    mix = jnp.dot(y.astype(BF16), wo_ref[...], preferred_element_type=F32)
    m = mod_ref[0]
    x1 = x_ref[...] + m[2:3] * mix
    x1_ref[...] = x1
    ms = jnp.mean(x1 * x1, axis=-1, keepdims=True)
    h2 = x1 * lax.rsqrt(ms + EPS) * nw_ref[...]
    h2 = h2 * (1.0 + m[4:5]) + m[3:4]
    h2_ref[...] = h2.astype(BF16)
    gates_ref[...] = _route(h2, rwt_ref, rb_ref)


def _mix(hm, hs, p, x2d, mod_l, wa, wb, wo, norm2_w, rwt, rb, seq):
    n = x2d.shape[0]
    tiles_per_seq = seq // TM_MIX
    tile = lambda i: (i, 0)
    const = lambda i: (0, 0)
    return pl.pallas_call(
        _mix_kernel,
        grid=(n // TM_MIX,),
        in_specs=[pl.BlockSpec((TM_MIX, D), tile),
                  pl.BlockSpec((TM_MIX, D), tile),
                  pl.BlockSpec((TM_MIX, D), lambda i: (i, P_GA)),
                  pl.BlockSpec((TM_MIX, D), lambda i: (i, P_GB)),
                  pl.BlockSpec((TM_MIX, D), tile),
                  pl.BlockSpec((1, 6, D), lambda i: (i // tiles_per_seq, 0, 0)),
                  pl.BlockSpec((D, D), const),
                  pl.BlockSpec((D, D), const),
                  pl.BlockSpec((D, D), const),
                  pl.BlockSpec((1, D), const),
                  pl.BlockSpec((N_EXPERTS, D), const),
                  pl.BlockSpec((N_EXPERTS, 1), const)],
        out_specs=[pl.BlockSpec((TM_MIX, D), tile),
                   pl.BlockSpec((TM_MIX, D), tile),
                   pl.BlockSpec((TM_MIX, LANES), tile)],
        out_shape=[jax.ShapeDtypeStruct((n, D), F32),
                   jax.ShapeDtypeStruct((n, D), BF16),
                   jax.ShapeDtypeStruct((n, LANES), F32)],
        compiler_params=_params(("parallel",)),
        name="mix_out",
    )(hm, hs, p, p, x2d, mod_l, wa, wb, wo, norm2_w.reshape(1, D), rwt, rb)


def _moe_kernel(h2_ref, gates_ref, x1_ref, mod_ref, wg_ref, wu_ref, wd_ref,
                sg_ref, su_ref, sd_ref, fnw_ref, o_ref, wg_b, wu_b, wd_b, *, final_norm):
    e = pl.program_id(1)
    chunks = [pl.ds(r * R_MOE, R_MOE) for r in range(TM_MOE // R_MOE)]

    def swiglu(rows, gate_cols):
        xr = h2_ref[rows, :]
        g = jnp.dot(xr, wg_b[...], preferred_element_type=F32)
        u = jnp.dot(xr, wu_b[...], preferred_element_type=F32)
        hid = g * _sigmoid(g) * u
        if gate_cols is not None:
            hid = hid * gate_cols
        return jnp.dot(hid.astype(BF16), wd_b[...], preferred_element_type=F32)

    @pl.when(e == 0)
    def _():
        wg_b[...] = sg_ref[0].astype(BF16)
        wu_b[...] = su_ref[0].astype(BF16)
        wd_b[...] = sd_ref[0].astype(BF16)
        for rows in chunks:
            o_ref[rows, :] = swiglu(rows, None)

    wg_b[:, :EXPERT_FF] = wg_ref[0, 0].astype(BF16)
    wg_b[:, EXPERT_FF:] = wg_ref[0, 1].astype(BF16)
    wu_b[:, :EXPERT_FF] = wu_ref[0, 0].astype(BF16)
    wu_b[:, EXPERT_FF:] = wu_ref[0, 1].astype(BF16)
    wd_b[:EXPERT_FF, :] = wd_ref[0, 0].astype(BF16)
    wd_b[EXPERT_FF:, :] = wd_ref[0, 1].astype(BF16)
    lane = lax.broadcasted_iota(jnp.int32, (1, LANES), 1)
    first_half = lax.broadcasted_iota(jnp.int32, (1, 2 * EXPERT_FF), 1) < EXPERT_FF
    for rows in chunks:
        gts = gates_ref[rows, :]
        g0 = jnp.sum(jnp.where(lane == 2 * e, gts, 0.0), axis=1, keepdims=True)
        g1 = jnp.sum(jnp.where(lane == 2 * e + 1, gts, 0.0), axis=1, keepdims=True)
        o_ref[rows, :] += swiglu(rows, jnp.where(first_half, g0, g1))

    @pl.when(e == pl.num_programs(1) - 1)
    def _():
        x2 = x1_ref[...] + mod_ref[0][5:6] * o_ref[...]
        if final_norm:
            ms = jnp.mean(x2 * x2, axis=-1, keepdims=True)
            x2 = x2 * lax.rsqrt(ms + EPS) * fnw_ref[...]
        o_ref[...] = x2


def _moe(h2, gates, x1, mod_l, layer, w_gate, w_up, w_down, sh_gate, sh_up, sh_down, fnw, seq, final_norm):
    n = h2.shape[0]
    tiles_per_seq = seq // TM_MOE
    tile = lambda i, e: (i, 0)
    once = pl.Buffered(1)
    return pl.pallas_call(
        functools.partial(_moe_kernel, final_norm=final_norm),
        grid=(n // TM_MOE, N_EXPERTS // 2),
        in_specs=[pl.BlockSpec((TM_MOE, D), tile, pipeline_mode=once),
                  pl.BlockSpec((TM_MOE, LANES), tile, pipeline_mode=once),
                  pl.BlockSpec((TM_MOE, D), tile, pipeline_mode=once),
                  pl.BlockSpec((1, 6, D), lambda i, e: (i // tiles_per_seq, 0, 0)),
                  pl.BlockSpec((1, 2, D, EXPERT_FF), lambda i, e: (layer, e, 0, 0)),
                  pl.BlockSpec((1, 2, D, EXPERT_FF), lambda i, e: (layer, e, 0, 0)),
                  pl.BlockSpec((1, 2, EXPERT_FF, D), lambda i, e: (layer, e, 0, 0)),
                  pl.BlockSpec((1, D, SHARED_FF), lambda i, e: (layer, 0, 0), pipeline_mode=once),
                  pl.BlockSpec((1, D, SHARED_FF), lambda i, e: (layer, 0, 0), pipeline_mode=once),
                  pl.BlockSpec((1, SHARED_FF, D), lambda i, e: (layer, 0, 0), pipeline_mode=once),
                  pl.BlockSpec((1, D), lambda i, e: (0, 0))],
        out_specs=pl.BlockSpec((TM_MOE, D), tile),
        out_shape=jax.ShapeDtypeStruct((n, D), F32),
        scratch_shapes=[pltpu.VMEM((D, 2 * EXPERT_FF), BF16),
                        pltpu.VMEM((D, 2 * EXPERT_FF), BF16),
                        pltpu.VMEM((2 * EXPERT_FF, D), BF16)],
        compiler_params=_params(("parallel", "arbitrary")),
        name="moe",
    )(h2, gates, x1, mod_l, w_gate, w_up, w_down, sh_gate, sh_up, sh_down, fnw.reshape(1, D))


def kernel(x, c, ada_w, ada_b, norm1_w, w_in, conv_w, conv_b, m_igate_b, m_fgate_b, m_norm_w, w_proj_a, w_proj_b, w_out, norm2_w, router_w, router_b, w_gate, w_up, w_down, sh_gate, sh_up, sh_down, final_norm_w):
    bsz, seq, _ = x.shape
    n = bsz * seq
    depth = ada_w.shape[0]
    assert SHARED_FF == 2 * EXPERT_FF and seq % TM_IN == 0 and seq % L_CHUNK == 0 and seq % TQ == 0 and seq >= WIN and WIN % KB == 0 and TQ % KB == 0

    c_pad = jnp.zeros((SUBLANES, D), F32).at[:bsz].set(c)
    mod = _ada(c_pad, ada_w, ada_b)[:, :bsz].reshape(depth, bsz, 6, D)

    gate_lo = 2 * N_MHEADS * DK + 2 * N_MHEADS * DV
    gate_hi = gate_lo + 2 * N_MHEADS
    xc = x.reshape(n, D)
    for l in range(depth):
        w_l = w_in[l]
        w_main = jnp.concatenate([w_l[:, :gate_lo], w_l[:, gate_hi:]], axis=1).astype(BF16)
        wg = w_l[:, gate_lo:gate_hi].astype(BF16)
        wg_col = jnp.zeros((D, LANES), BF16).at[:, :2 * N_MHEADS].set(wg)
        gate_b = jnp.concatenate([m_igate_b[l], m_fgate_b[l]]).astype(F32)
        brow = jnp.zeros((1, LANES), F32).at[0, :2 * N_MHEADS].set(gate_b)

        p, gcol, grow = _inproj(xc, mod[l], norm1_w[l], w_main, wg_col, wg.T, seq)
        hm = _mlstm(p, gcol, grow, conv_w[l], conv_b[l], brow, gate_b.reshape(2 * N_MHEADS, 1),
                    m_norm_w[l], bsz, seq)
        hs = _sb_attention(p, bsz, seq)
        x1, h2, gates = _mix(hm, hs, p, xc, mod[l], w_proj_a[l].astype(BF16), w_proj_b[l].astype(BF16),
                             w_out[l].astype(BF16), norm2_w[l], router_w[l].T,
                             router_b[l].reshape(N_EXPERTS, 1), seq)
        xc = _moe(h2, gates, x1, mod[l], l, w_gate, w_up, w_down, sh_gate, sh_up, sh_down,
                  final_norm_w, seq, final_norm=(l == depth - 1))
    return xc.reshape(bsz, seq, D)
```

```python
import functools

import jax
import jax.numpy as jnp
from jax import lax
from jax.experimental import pallas as pl
from jax.experimental.pallas import tpu as pltpu

F32 = jnp.float32
BF16 = jnp.bfloat16

D = 1024
DEPTH = 2
N_MHEADS = 4
DK = 128
DV = 256
CONV_K = 4
SB_HEADS = 16
SB_DH = 64
N_EXPERTS = 64
TOP_K = 8
N_GROUPS = 8
TOPK_GROUPS = 4
GROUP_SIZE = N_EXPERTS // N_GROUPS
EXPERT_FF = 128
SHARED_FF = 256
ROUTED_SCALE = 2.5
EPS = 1e-6

LANES = 128
SUBLANES = 8
VMEM_LIMIT = 56 * 1024 * 1024

P_MQK, P_MV, P_MO, P_SQ, P_SK, P_SV, P_GA, P_GB = range(8)

TM_IN = 2048
TN_IN = 1024
L_CHUNK = 256
TQ = 512
KB = 128
WIN = 384
SKIP_LOG = -104.0
TM_MIX = 512
TM_MOE = 2048
R_MOE = 1024
NEG_INF = float("-inf")
LOG2E = 1.4426950408889634


def _sigmoid(x):
    return 1.0 / (1.0 + jnp.exp(-x))


def _softplus(x):
    return jnp.maximum(x, 0.0) + jnp.log(1.0 + jnp.exp2(jnp.abs(x) * (-LOG2E)))


def _params(sem):
    return pltpu.CompilerParams(dimension_semantics=sem, vmem_limit_bytes=VMEM_LIMIT)


def _ada_kernel(c_ref, w_ref, b_ref, o_ref):
    c = c_ref[...]
    cond = c * _sigmoid(c)
    o_ref[0] = jnp.dot(cond, w_ref[0], preferred_element_type=F32,
                       precision=lax.Precision.HIGHEST) + b_ref[0]


def _ada(c_pad, ada_w, ada_b):
    tn = 1536
    depth = ada_w.shape[0]
    return pl.pallas_call(
        _ada_kernel,
        grid=(depth, 6 * D // tn),
        in_specs=[pl.BlockSpec((SUBLANES, D), lambda l, j: (0, 0)),
                  pl.BlockSpec((1, D, tn), lambda l, j: (l, 0, j)),
                  pl.BlockSpec((1, 1, tn), lambda l, j: (l, 0, j))],
        out_specs=pl.BlockSpec((1, SUBLANES, tn), lambda l, j: (l, 0, j)),
        out_shape=jax.ShapeDtypeStruct((depth, SUBLANES, 6 * D), F32),
        compiler_params=_params(("parallel", "parallel")),
        name="ada_mod",
    )(c_pad, ada_w, ada_b.reshape(depth, 1, 6 * D))


def _inproj_kernel(x_ref, mod_ref, nw_ref, w_ref, wgc_ref, wgr_ref,
                   p_ref, gcol_ref, grow_ref, h_scr):
    @pl.when(pl.program_id(1) == 0)
    def _():
        x = x_ref[...]
        ms = jnp.mean(x * x, axis=-1, keepdims=True)
        y = x * lax.rsqrt(ms + EPS) * nw_ref[...]
        m = mod_ref[0]
        hb = (y * (1.0 + m[1:2]) + m[0:1]).astype(BF16)
        h_scr[...] = hb
        gcol_ref[...] = jnp.dot(hb, wgc_ref[...], preferred_element_type=F32)
        grow_ref[...] = lax.dot_general(wgr_ref[...], hb, (((1,), (1,)), ((), ())),
                                        preferred_element_type=F32)

    p_ref[...] = jnp.dot(h_scr[...], w_ref[...], preferred_element_type=F32).astype(BF16)


def _inproj(x2d, mod_l, norm_w, w_main, wg_col, wg_row, seq):
    n = x2d.shape[0]
    tiles_per_seq = seq // TM_IN
    return pl.pallas_call(
        _inproj_kernel,
        grid=(n // TM_IN, w_main.shape[1] // TN_IN),
        in_specs=[pl.BlockSpec((TM_IN, D), lambda i, j: (i, 0)),
                  pl.BlockSpec((1, 6, D), lambda i, j: (i // tiles_per_seq, 0, 0)),
                  pl.BlockSpec((1, D), lambda i, j: (0, 0)),
                  pl.BlockSpec((D, TN_IN), lambda i, j: (0, j)),
                  pl.BlockSpec((D, LANES), lambda i, j: (0, 0)),
                  pl.BlockSpec((SUBLANES, D), lambda i, j: (0, 0))],
        out_specs=[pl.BlockSpec((TM_IN, TN_IN), lambda i, j: (i, j)),
                   pl.BlockSpec((TM_IN, LANES), lambda i, j: (i, 0)),
                   pl.BlockSpec((SUBLANES, TM_IN), lambda i, j: (0, i))],
        out_shape=[jax.ShapeDtypeStruct((n, w_main.shape[1]), BF16),
                   jax.ShapeDtypeStruct((n, LANES), F32),
                   jax.ShapeDtypeStruct((SUBLANES, n), F32)],
        scratch_shapes=[pltpu.VMEM((TM_IN, D), BF16)],
        compiler_params=_params(("parallel", "arbitrary")),
        name="in_proj",
    )(x2d, mod_l, norm_w.reshape(1, D), w_main, wg_col, wg_row)


def _mlstm_kernel(mqk_ref, mv_ref, mo_ref, gcol_ref, grow_ref, cw_ref, cb_ref,
                  brow_ref, bcol_ref, nw_ref, o_ref, qk_buf, c_scr, m_scr):
    L = L_CHUNK

    @pl.when(pl.program_id(1) == 0)
    def _():
        qk_buf[0:SUBLANES, :] = jnp.zeros((SUBLANES, D), F32)
        c_scr[...] = jnp.zeros_like(c_scr)
        m_scr[...] = jnp.zeros_like(m_scr)

    qk_buf[SUBLANES:SUBLANES + L, :] = mqk_ref[...].astype(F32)
    conv = cb_ref[...] + jnp.zeros((L, D), F32)
    for j in range(CONV_K):
        off = SUBLANES - (CONV_K - 1) + j
        conv = conv + qk_buf[off:off + L, :] * cw_ref[j:j + 1, :]
    qk_buf[0:SUBLANES, :] = qk_buf[L:L + SUBLANES, :]
    qk = conv * _sigmoid(conv)

    gcol = gcol_ref[...] + brow_ref[...]
    grow = grow_ref[...] + bcol_ref[...]
    lf_col_all = -_softplus(-gcol)
    lf_row_all = -_softplus(-grow)

    r_idx = lax.broadcasted_iota(jnp.int32, (L, L), 0)
    c_idx = lax.broadcasted_iota(jnp.int32, (L, L), 1)
    causal = c_idx <= r_idx
    one_col = jnp.where(lax.broadcasted_iota(jnp.int32, (L, LANES), 1) == 0, 1.0, 0.0).astype(BF16)

    for h in range(N_MHEADS):
        ig_row = grow[h:h + 1, :]
        lf_row = lf_row_all[N_MHEADS + h:N_MHEADS + h + 1, :]
        ig_col = gcol[:, h:h + 1]
        lf_col = lf_col_all[:, N_MHEADS + h:N_MHEADS + h + 1]
        m_prev = m_scr[h:h + 1, 0:1]

        b_row = jnp.sum(jnp.where(c_idx >= r_idx, lf_col, 0.0), axis=0, keepdims=True)
        b_col = jnp.sum(jnp.where(causal, lf_row, 0.0), axis=1, keepdims=True)
        u_row = ig_row - b_row
        u_col = ig_col - b_col
        cm_col = jnp.maximum(m_prev, jnp.max(jnp.where(causal, u_row, NEG_INF), axis=1, keepdims=True))
        cm_last = jnp.maximum(m_prev, jnp.max(u_row, axis=1, keepdims=True))
        a_tot = jnp.sum(lf_row, axis=1, keepdims=True)

        w_intra = jnp.where(causal, jnp.exp(u_row - cm_col), 0.0)
        w_inter = jnp.exp(m_prev - cm_col)
        clamp = jnp.exp(-(b_col + cm_col))
        wc_col = jnp.exp(u_col - cm_last)
        decay = jnp.exp(m_prev - cm_last)

        q_h = (qk[:, h * DK:(h + 1) * DK] * (DK ** -0.5)).astype(BF16)
        k_f = qk[:, N_MHEADS * DK + h * DK:N_MHEADS * DK + (h + 1) * DK]
        k_h = k_f.astype(BF16)
        v_aug = jnp.concatenate([mv_ref[:, h * DV:(h + 1) * DV], one_col], axis=1)

        s = lax.dot_general(q_h, k_h, (((1,), (1,)), ((), ())), preferred_element_type=F32)
        s = (s * w_intra).astype(BF16)
        cn = c_scr[h]
        nd = jnp.dot(s, v_aug, preferred_element_type=F32)
        nd = nd + w_inter * jnp.dot(q_h, cn.astype(BF16), preferred_element_type=F32)
        num = nd[:, :DV]
        den = nd[:, DV:DV + 1]
        hh = num / jnp.maximum(jnp.abs(den), clamp)

        kw_t = (k_f * wc_col).T.astype(BF16)
        c_scr[h] = decay * cn + jnp.dot(kw_t, v_aug, preferred_element_type=F32)
        m_scr[h:h + 1, :] = jnp.broadcast_to(a_tot + cm_last, (1, LANES))

        hn = hh * lax.rsqrt(jnp.mean(hh * hh, axis=-1, keepdims=True) + EPS)
        hn = hn * nw_ref[:, h * DV:(h + 1) * DV]
        og = _sigmoid(mo_ref[:, h * DV:(h + 1) * DV].astype(F32))
        o_ref[:, h * DV:(h + 1) * DV] = (hn * og).astype(BF16)


def _mlstm(p, gcol, grow, conv_w, conv_b, brow, bcol, m_norm_w, bsz, seq):
    n = p.shape[0]
    nc = seq // L_CHUNK
    row = lambda b, c: b * nc + c
    return pl.pallas_call(
        _mlstm_kernel,
        grid=(bsz, nc),
        in_specs=[pl.BlockSpec((L_CHUNK, D), lambda b, c: (row(b, c), P_MQK)),
                  pl.BlockSpec((L_CHUNK, D), lambda b, c: (row(b, c), P_MV)),
                  pl.BlockSpec((L_CHUNK, D), lambda b, c: (row(b, c), P_MO)),
                  pl.BlockSpec((L_CHUNK, LANES), lambda b, c: (row(b, c), 0)),
                  pl.BlockSpec((SUBLANES, L_CHUNK), lambda b, c: (0, row(b, c))),
                  pl.BlockSpec((CONV_K, D), lambda b, c: (0, 0)),
                  pl.BlockSpec((1, D), lambda b, c: (0, 0)),
                  pl.BlockSpec((1, LANES), lambda b, c: (0, 0)),
                  pl.BlockSpec((SUBLANES, 1), lambda b, c: (0, 0)),
                  pl.BlockSpec((1, D), lambda b, c: (0, 0))],
        out_specs=pl.BlockSpec((L_CHUNK, D), lambda b, c: (row(b, c), 0)),
        out_shape=jax.ShapeDtypeStruct((n, D), BF16),
        scratch_shapes=[pltpu.VMEM((L_CHUNK + 2 * SUBLANES, D), F32),
                        pltpu.VMEM((N_MHEADS, DK, DV + LANES), F32),
                        pltpu.VMEM((SUBLANES, LANES), F32)],
        compiler_params=_params(("parallel", "arbitrary")),
        name="mlstm",
    )(p, p, p, gcol, grow, conv_w, conv_b.reshape(1, D), brow, bcol, m_norm_w.reshape(1, D))


def _sb_cum(sp_blocks, tri):
    rows = []
    for sp in sp_blocks:
        hi = sp.astype(BF16)
        lo = (sp - hi.astype(F32)).astype(BF16)
        rows.append(jnp.concatenate([hi, lo], axis=1))
    lhs = rows[0] if len(rows) == 1 else jnp.concatenate(rows, axis=0)
    cum = jnp.dot(lhs, tri, preferred_element_type=F32)
    return [cum[r * KB:(r + 1) * KB] for r in range(len(sp_blocks))]


def _nt_dot(a, b):
    return lax.dot_general(a, b, (((1,), (1,)), ((), ())), preferred_element_type=F32)


def _sb_tail(qm2, accs, carries, n_rem, k_ref, v_ref, tri):
    def live(c0, c1):
        return (jnp.max(jnp.maximum(c0, c1)) > SKIP_LOG).astype(jnp.int32)

    def cond(st):
        return jnp.logical_and(st[0] < n_rem, st[1] > 0)

    def body(st):
        j, _, a0, a1, c0, c1 = st
        k0 = pl.multiple_of((n_rem - 1 - j) * KB, KB)
        k_blk = k_ref[pl.ds(k0, KB), :]
        v_blk = v_ref[pl.ds(k0, KB), :]
        z_b = [_nt_dot(qm, k_blk) for qm in qm2]
        cum_b = _sb_cum([_softplus(z) for z in z_b], tri)
        acc_b, car_b = [a0, a1], [c0, c1]
        for hd in range(2):
            att = jnp.exp(z_b[hd] + cum_b[hd][:, :KB] + car_b[hd]).astype(BF16)
            acc_b[hd] = acc_b[hd] + jnp.dot(att, v_blk, preferred_element_type=F32)
            car_b[hd] = car_b[hd] + cum_b[hd][:, KB:]
        return (j + 1, live(car_b[0], car_b[1]), acc_b[0], acc_b[1], car_b[0], car_b[1])

    st = lax.while_loop(cond, body, (jnp.int32(0), live(carries[0], carries[1]),
                                     accs[0], accs[1], carries[0], carries[1]))
    return st[2], st[3]


def _sb_kernel(q_ref, k_ref, v_ref, tri_ref, o_ref):
    i = pl.program_id(2)
    tri = tri_ref[...]
    lane = lax.broadcasted_iota(jnp.int32, (1, LANES), 1)
    q2 = q_ref[...] * jnp.asarray(SB_DH ** -0.5, BF16)
    qms = [jnp.where((lane >= hd * SB_DH) & (lane < (hd + 1) * SB_DH), q2, jnp.zeros_like(q2))
           for hd in range(2)]
    nsub = TQ // KB
    nb = WIN // KB
    diff = (lax.broadcasted_iota(jnp.int32, (KB, WIN), 1)
            - lax.broadcasted_iota(jnp.int32, (KB, WIN), 0))

    zs, sps, valids, v_wins, n_rems = [], [], [], [], []
    for s in range(nsub):
        qs = i * TQ + s * KB
        ws = pl.multiple_of(jnp.maximum(qs - (WIN - KB), 0), KB)
        valid = diff < (qs - ws)
        k_win = k_ref[pl.ds(ws, WIN), :]
        for hd in range(2):
            z = _nt_dot(qms[hd][s * KB:(s + 1) * KB], k_win)
            zs.append(z)
            sps.append(jnp.where(valid, _softplus(z), 0.0))
        valids.append(valid)
        v_wins.append(v_ref[pl.ds(ws, WIN), :])
        n_rems.append(ws // KB)
    cums = _sb_cum([sp[:, kb * KB:(kb + 1) * KB] for sp in sps for kb in range(nb)], tri)

    accs, carries = [], []
    for u in range(2 * nsub):
        carry = jnp.zeros((KB, KB), F32)
        logits = [None] * nb
        for kb in reversed(range(nb)):
            cum = cums[u * nb + kb]
            logits[kb] = zs[u][:, kb * KB:(kb + 1) * KB] + cum[:, :KB] + carry
            carry = carry + cum[:, KB:]
        att = jnp.where(valids[u // 2], jnp.exp(jnp.concatenate(logits, axis=1)), 0.0).astype(BF16)
        accs.append(jnp.dot(att, v_wins[u // 2], preferred_element_type=F32))
        carries.append(carry)

    worst = functools.reduce(jnp.maximum, carries)
    need_tail = jnp.logical_and(jnp.max(worst) > SKIP_LOG, n_rems[-1] > 0)

    def run_tails():
        res = []
        for s in range(nsub):
            qm2 = [qms[hd][s * KB:(s + 1) * KB] for hd in range(2)]
            res.extend(_sb_tail(qm2, accs[2 * s:2 * s + 2], carries[2 * s:2 * s + 2],
                                n_rems[s], k_ref, v_ref, tri))
        return tuple(res)

    accs = lax.cond(need_tail, run_tails, lambda: tuple(accs))
    o_ref[...] = jnp.concatenate(
        [jnp.where(lane < SB_DH, accs[2 * s], accs[2 * s + 1]).astype(BF16) for s in range(nsub)], axis=0)


def _sb_attention(p, bsz, seq):
    n = p.shape[0]
    nq = seq // TQ
    pairs = SB_HEADS * SB_DH // LANES
    per_blk = D // LANES
    rr = lax.broadcasted_iota(jnp.int32, (2 * KB, 2 * KB), 0) % KB
    cc = lax.broadcasted_iota(jnp.int32, (2 * KB, 2 * KB), 1)
    tri = jnp.where((cc >= KB) | (rr >= cc), -1.0, 0.0).astype(BF16)
    return pl.pallas_call(
        _sb_kernel,
        grid=(bsz, pairs, nq),
        in_specs=[pl.BlockSpec((TQ, LANES), lambda b, g, i: (b * nq + i, P_SQ * per_blk + g)),
                  pl.BlockSpec((seq, LANES), lambda b, g, i: (b, P_SK * per_blk + g)),
                  pl.BlockSpec((seq, LANES), lambda b, g, i: (b, P_SV * per_blk + g)),
                  pl.BlockSpec((2 * KB, 2 * KB), lambda b, g, i: (0, 0))],
        out_specs=pl.BlockSpec((TQ, LANES), lambda b, g, i: (b * nq + i, g)),
        out_shape=jax.ShapeDtypeStruct((n, D), BF16),
        compiler_params=_params(("parallel", "parallel", "arbitrary")),
        name="sb_attention",
    )(p, p, p, tri)


def _route(h2, rwt_ref, rb_ref):
    tm = h2.shape[0]
    logits = lax.dot_general(rwt_ref[...], h2, (((1,), (1,)), ((), ())),
                             preferred_element_type=F32, precision=lax.Precision.HIGHEST)
    scores = _sigmoid(logits)
    sel = scores + rb_ref[...]
    sub = lax.broadcasted_iota(jnp.int32, (GROUP_SIZE, tm), 0).astype(F32)
    grp_scores = []
    for g in range(N_GROUPS):
        x = sel[g * GROUP_SIZE:(g + 1) * GROUP_SIZE, :]
        m1 = jnp.max(x, axis=0, keepdims=True)
        first = jnp.min(jnp.where(x == m1, sub, float(GROUP_SIZE)), axis=0, keepdims=True)
        m2 = jnp.max(jnp.where(sub == first, NEG_INF, x), axis=0, keepdims=True)
        grp_scores.append(m1 + m2)
    gs = jnp.concatenate(grp_scores, axis=0)
    g_iota = lax.broadcasted_iota(jnp.int32, (N_GROUPS, tm), 0)
    g_rank = jnp.zeros((N_GROUPS, tm), F32)
    for g in range(N_GROUPS):
        row = gs[g:g + 1, :]
        g_rank = g_rank + jnp.where(row > gs, 1.0, jnp.where(row == gs, jnp.where(g_iota > g, 1.0, 0.0), 0.0))
    g_sel = g_rank < TOPK_GROUPS
    masked = jnp.concatenate(
        [jnp.where(g_sel[g:g + 1, :], sel[g * GROUP_SIZE:(g + 1) * GROUP_SIZE, :], NEG_INF)
         for g in range(N_GROUPS)], axis=0)
    e_iota = lax.broadcasted_iota(jnp.int32, (N_EXPERTS, tm), 0)
    e_rank = jnp.zeros((N_EXPERTS, tm), F32)
    for e in range(N_EXPERTS):
        row = masked[e:e + 1, :]
        e_rank = e_rank + jnp.where(row > masked, 1.0,
                                    jnp.where(row == masked, jnp.where(e_iota > e, 1.0, 0.0), 0.0))
    w_sel = jnp.where(e_rank < TOP_K, scores, 0.0)
    gates_t = w_sel / jnp.sum(w_sel, axis=0, keepdims=True) * ROUTED_SCALE
    gates_t = jnp.concatenate([gates_t, jnp.zeros((LANES - N_EXPERTS, tm), F32)], axis=0)
    return gates_t.T


def _mix_kernel(hm_ref, hs_ref, ga_ref, gb_ref, x_ref, mod_ref, wa_ref, wb_ref, wo_ref,
                nw_ref, rwt_ref, rb_ref, x1_ref, h2_ref, gates_ref):
    ya = jnp.dot(hm_ref[...], wa_ref[...], preferred_element_type=F32)
    yb = jnp.dot(hs_ref[...], wb_ref[...], preferred_element_type=F32)
    y = _sigmoid(ga_ref[...].astype(F32)) * ya + _sigmoid(gb_ref[...].astype(F32)) * yb
    mix = jnp.dot(y.astype(BF16), wo_ref[...], preferred_element_type=F32)
    m = mod_ref[0]
    x1 = x_ref[...] + m[2:3] * mix
    x1_ref[...] = x1
    ms = jnp.mean(x1 * x1, axis=-1, keepdims=True)
    h2 = x1 * lax.rsqrt(ms + EPS) * nw_ref[...]
    h2 = h2 * (1.0 + m[4:5]) + m[3:4]
    h2_ref[...] = h2.astype(BF16)
    gates_ref[...] = _route(h2, rwt_ref, rb_ref)


def _mix(hm, hs, p, x2d, mod_l, wa, wb, wo, norm2_w, rwt, rb, seq):
    n = x2d.shape[0]
    tiles_per_seq = seq // TM_MIX
    tile = lambda i: (i, 0)
    const = lambda i: (0, 0)
    return pl.pallas_call(
        _mix_kernel,
        grid=(n // TM_MIX,),
        in_specs=[pl.BlockSpec((TM_MIX, D), tile),
                  pl.BlockSpec((TM_MIX, D), tile),
                  pl.BlockSpec((TM_MIX, D), lambda i: (i, P_GA)),
                  pl.BlockSpec((TM_MIX, D), lambda i: (i, P_GB)),
                  pl.BlockSpec((TM_MIX, D), tile),
                  pl.BlockSpec((1, 6, D), lambda i: (i // tiles_per_seq, 0, 0)),
                  pl.BlockSpec((D, D), const),
                  pl.BlockSpec((D, D), const),
                  pl.BlockSpec((D, D), const),
                  pl.BlockSpec((1, D), const),
                  pl.BlockSpec((N_EXPERTS, D), const),
                  pl.BlockSpec((N_EXPERTS, 1), const)],
        out_specs=[pl.BlockSpec((TM_MIX, D), tile),
                   pl.BlockSpec((TM_MIX, D), tile),
                   pl.BlockSpec((TM_MIX, LANES), tile)],
        out_shape=[jax.ShapeDtypeStruct((n, D), F32),
                   jax.ShapeDtypeStruct((n, D), BF16),
                   jax.ShapeDtypeStruct((n, LANES), F32)],
        compiler_params=_params(("parallel",)),
        name="mix_out",
    )(hm, hs, p, p, x2d, mod_l, wa, wb, wo, norm2_w.reshape(1, D), rwt, rb)


def _moe_kernel(h2_ref, gates_ref, x1_ref, mod_ref, wg_ref, wu_ref, wd_ref,
                sg_ref, su_ref, sd_ref, fnw_ref, o_ref, wg_b, wu_b, wd_b, *, final_norm):
    e = pl.program_id(1)
    chunks = [pl.ds(r * R_MOE, R_MOE) for r in range(TM_MOE // R_MOE)]

    def swiglu(rows, gate_cols):
        xr = h2_ref[rows, :]
        g = jnp.dot(xr, wg_b[...], preferred_element_type=F32)
        u = jnp.dot(xr, wu_b[...], preferred_element_type=F32)
        hid = g * _sigmoid(g) * u
        if gate_cols is not None:
            hid = hid * gate_cols
        return jnp.dot(hid.astype(BF16), wd_b[...], preferred_element_type=F32)

    @pl.when(e == 0)
    def _():
        wg_b[...] = sg_ref[0].astype(BF16)
        wu_b[...] = su_ref[0].astype(BF16)
        wd_b[...] = sd_ref[0].astype(BF16)
        for rows in chunks:
            o_ref[rows, :] = swiglu(rows, None)

    wg_b[:, :EXPERT_FF] = wg_ref[0, 0].astype(BF16)
    wg_b[:, EXPERT_FF:] = wg_ref[0, 1].astype(BF16)
    wu_b[:, :EXPERT_FF] = wu_ref[0, 0].astype(BF16)
    wu_b[:, EXPERT_FF:] = wu_ref[0, 1].astype(BF16)
    wd_b[:EXPERT_FF, :] = wd_ref[0, 0].astype(BF16)
    wd_b[EXPERT_FF:, :] = wd_ref[0, 1].astype(BF16)
    lane = lax.broadcasted_iota(jnp.int32, (1, LANES), 1)
    first_half = lax.broadcasted_iota(jnp.int32, (1, 2 * EXPERT_FF), 1) < EXPERT_FF
    for rows in chunks:
        gts = gates_ref[rows, :]
        g0 = jnp.sum(jnp.where(lane == 2 * e, gts, 0.0), axis=1, keepdims=True)
        g1 = jnp.sum(jnp.where(lane == 2 * e + 1, gts, 0.0), axis=1, keepdims=True)
        o_ref[rows, :] += swiglu(rows, jnp.where(first_half, g0, g1))

    @pl.when(e == pl.num_programs(1) - 1)
    def _():
        x2 = x1_ref[...] + mod_ref[0][5:6] * o_ref[...]
        if final_norm:
            ms = jnp.mean(x2 * x2, axis=-1, keepdims=True)
            x2 = x2 * lax.rsqrt(ms + EPS) * fnw_ref[...]
        o_ref[...] = x2


def _moe(h2, gates, x1, mod_l, layer, w_gate, w_up, w_down, sh_gate, sh_up, sh_down, fnw, seq, final_norm):
    n = h2.shape[0]
    tiles_per_seq = seq // TM_MOE
    tile = lambda i, e: (i, 0)
    once = pl.Buffered(1)
    return pl.pallas_call(
        functools.partial(_moe_kernel, final_norm=final_norm),
        grid=(n // TM_MOE, N_EXPERTS // 2),
        in_specs=[pl.BlockSpec((TM_MOE, D), tile, pipeline_mode=once),
                  pl.BlockSpec((TM_MOE, LANES), tile, pipeline_mode=once),
                  pl.BlockSpec((TM_MOE, D), tile, pipeline_mode=once),
                  pl.BlockSpec((1, 6, D), lambda i, e: (i // tiles_per_seq, 0, 0)),
                  pl.BlockSpec((1, 2, D, EXPERT_FF), lambda i, e: (layer, e, 0, 0)),
                  pl.BlockSpec((1, 2, D, EXPERT_FF), lambda i, e: (layer, e, 0, 0)),
                  pl.BlockSpec((1, 2, EXPERT_FF, D), lambda i, e: (layer, e, 0, 0)),
                  pl.BlockSpec((1, D, SHARED_FF), lambda i, e: (layer, 0, 0), pipeline_mode=once),
                  pl.BlockSpec((1, D, SHARED_FF), lambda i, e: (layer, 0, 0), pipeline_mode=once),
                  pl.BlockSpec((1, SHARED_FF, D), lambda i, e: (layer, 0, 0), pipeline_mode=once),
                  pl.BlockSpec((1, D), lambda i, e: (0, 0))],
        out_specs=pl.BlockSpec((TM_MOE, D), tile),
        out_shape=jax.ShapeDtypeStruct((n, D), F32),
        scratch_shapes=[pltpu.VMEM((D, 2 * EXPERT_FF), BF16),
                        pltpu.VMEM((D, 2 * EXPERT_FF), BF16),
                        pltpu.VMEM((2 * EXPERT_FF, D), BF16)],
        compiler_params=_params(("parallel", "arbitrary")),
        name="moe",
    )(h2, gates, x1, mod_l, w_gate, w_up, w_down, sh_gate, sh_up, sh_down, fnw.reshape(1, D))


def kernel(x, c, ada_w, ada_b, norm1_w, w_in, conv_w, conv_b, m_igate_b, m_fgate_b, m_norm_w, w_proj_a, w_proj_b, w_out, norm2_w, router_w, router_b, w_gate, w_up, w_down, sh_gate, sh_up, sh_down, final_norm_w):
    bsz, seq, _ = x.shape
    n = bsz * seq
    depth = ada_w.shape[0]
    assert SHARED_FF == 2 * EXPERT_FF and seq % TM_IN == 0 and seq % L_CHUNK == 0 and seq % TQ == 0 and seq >= WIN and WIN % KB == 0 and TQ % KB == 0

    c_pad = jnp.zeros((SUBLANES, D), F32).at[:bsz].set(c)
    mod = _ada(c_pad, ada_w, ada_b)[:, :bsz].reshape(depth, bsz, 6, D)

    gate_lo = 2 * N_MHEADS * DK + 2 * N_MHEADS * DV
    gate_hi = gate_lo + 2 * N_MHEADS
    xc = x.reshape(n, D)
    for l in range(depth):
        w_main = jnp.concatenate([w_in[l, :, :gate_lo].astype(BF16), w_in[l, :, gate_hi:].astype(BF16)], axis=1)
        wg = w_in[l, :, gate_lo:gate_hi].astype(BF16)
        wg_col = jnp.zeros((D, LANES), BF16).at[:, :2 * N_MHEADS].set(wg)
        gate_b = jnp.concatenate([m_igate_b[l], m_fgate_b[l]]).astype(F32)
        brow = jnp.zeros((1, LANES), F32).at[0, :2 * N_MHEADS].set(gate_b)

        p, gcol, grow = _inproj(xc, mod[l], norm1_w[l], w_main, wg_col, wg.T, seq)
        hm = _mlstm(p, gcol, grow, conv_w[l], conv_b[l], brow, gate_b.reshape(2 * N_MHEADS, 1),
                    m_norm_w[l], bsz, seq)
        hs = _sb_attention(p, bsz, seq)
        x1, h2, gates = _mix(hm, hs, p, xc, mod[l], w_proj_a[l].astype(BF16), w_proj_b[l].astype(BF16),
                             w_out[l].astype(BF16), norm2_w[l], router_w[l].T,
                             router_b[l].reshape(N_EXPERTS, 1), seq)
        xc = _moe(h2, gates, x1, mod[l], l, w_gate, w_up, w_down, sh_gate, sh_up, sh_down,
                  final_norm_w, seq, final_norm=(l == depth - 1))
    return xc.reshape(bsz, seq, D)
```

```python
import functools

import jax
import jax.numpy as jnp
from jax import lax
from jax.experimental import pallas as pl
from jax.experimental.pallas import tpu as pltpu

F32 = jnp.float32
BF16 = jnp.bfloat16

D = 1024
DEPTH = 2
N_MHEADS = 4
DK = 128
DV = 256
CONV_K = 4
SB_HEADS = 16
SB_DH = 64
N_EXPERTS = 64
TOP_K = 8
N_GROUPS = 8
TOPK_GROUPS = 4
GROUP_SIZE = N_EXPERTS // N_GROUPS
EXPERT_FF = 128
SHARED_FF = 256
ROUTED_SCALE = 2.5
EPS = 1e-6

LANES = 128
SUBLANES = 8
VMEM_LIMIT = 56 * 1024 * 1024

P_MQK, P_MV, P_MO, P_SQ, P_SK, P_SV, P_GA, P_GB = range(8)

TM_IN = 2048
TN_IN = 1024
L_CHUNK = 256
TQ = 512
KB = 128
WIN = 384
SKIP_LOG = -104.0
TM_MIX = 512
TM_MOE = 2048
R_MOE = 1024
NEG_INF = float("-inf")
LOG2E = 1.4426950408889634


def _sigmoid(x):
    return 0.5 * jnp.tanh(0.5 * x) + 0.5


def _softplus(x):
    return jnp.maximum(x, 0.0) + jnp.log(1.0 + jnp.exp2(jnp.abs(x) * (-LOG2E)))


def _params(sem):
    return pltpu.CompilerParams(dimension_semantics=sem, vmem_limit_bytes=VMEM_LIMIT)


def _ada_kernel(c_ref, w_ref, b_ref, o_ref):
    c = c_ref[...]
    cond = c * _sigmoid(c)
    o_ref[0] = jnp.dot(cond, w_ref[0], preferred_element_type=F32,
                       precision=lax.Precision.HIGHEST) + b_ref[0]


def _ada(c_pad, ada_w, ada_b):
    tn = 1536
    depth = ada_w.shape[0]
    return pl.pallas_call(
        _ada_kernel,
        grid=(depth, 6 * D // tn),
        in_specs=[pl.BlockSpec((SUBLANES, D), lambda l, j: (0, 0)),
                  pl.BlockSpec((1, D, tn), lambda l, j: (l, 0, j)),
                  pl.BlockSpec((1, 1, tn), lambda l, j: (l, 0, j))],
        out_specs=pl.BlockSpec((1, SUBLANES, tn), lambda l, j: (l, 0, j)),
        out_shape=jax.ShapeDtypeStruct((depth, SUBLANES, 6 * D), F32),
        compiler_params=_params(("parallel", "parallel")),
        name="ada_mod",
    )(c_pad, ada_w, ada_b.reshape(depth, 1, 6 * D))


def _repack_kernel(w_ref, main_ref, gcol_ref, grow_ref, *, gate_lo, n_gate):
    w = w_ref[0]
    width = w.shape[1]
    main_ref[:, :gate_lo] = w[:, :gate_lo].astype(BF16)
    main_ref[:, gate_lo:] = w[:, gate_lo + n_gate:width].astype(BF16)
    lane = lax.broadcasted_iota(jnp.int32, (1, LANES), 1)
    g = jnp.where(lane < n_gate, w[:, gate_lo:gate_lo + LANES], 0.0)
    gcol_ref[...] = g.astype(BF16)
    grow_ref[...] = g.T[:SUBLANES, :].astype(BF16)


def _repack_w_in(w_in, layer, gate_lo, n_gate):
    depth, d, width = w_in.shape
    tr = 256
    return pl.pallas_call(
        functools.partial(_repack_kernel, gate_lo=gate_lo, n_gate=n_gate),
        grid=(d // tr,),
        in_specs=[pl.BlockSpec((1, tr, width), lambda i: (layer, i, 0))],
        out_specs=[pl.BlockSpec((tr, width - n_gate), lambda i: (i, 0)),
                   pl.BlockSpec((tr, LANES), lambda i: (i, 0)),
                   pl.BlockSpec((SUBLANES, tr), lambda i: (0, i))],
        out_shape=[jax.ShapeDtypeStruct((d, width - n_gate), BF16),
                   jax.ShapeDtypeStruct((d, LANES), BF16),
                   jax.ShapeDtypeStruct((SUBLANES, d), BF16)],
        compiler_params=_params(("parallel",)),
        name="repack_w_in",
    )(w_in)


def _inproj_kernel(x_ref, mod_ref, nw_ref, w_ref, wgc_ref, wgr_ref,
                   p_ref, gcol_ref, grow_ref, h_scr):
    @pl.when(pl.program_id(1) == 0)
    def _():
        x = x_ref[...]
        ms = jnp.mean(x * x, axis=-1, keepdims=True)
        y = x * lax.rsqrt(ms + EPS) * nw_ref[...]
        m = mod_ref[0]
        hb = (y * (1.0 + m[1:2]) + m[0:1]).astype(BF16)
        h_scr[...] = hb
        gcol_ref[...] = jnp.dot(hb, wgc_ref[...], preferred_element_type=F32)
        grow_ref[...] = lax.dot_general(wgr_ref[...], hb, (((1,), (1,)), ((), ())),
                                        preferred_element_type=F32)

    p_ref[...] = jnp.dot(h_scr[...], w_ref[...], preferred_element_type=F32).astype(BF16)


def _inproj(x2d, mod_l, norm_w, w_main, wg_col, wg_row, seq):
    n = x2d.shape[0]
    tiles_per_seq = seq // TM_IN
    return pl.pallas_call(
        _inproj_kernel,
        grid=(n // TM_IN, w_main.shape[1] // TN_IN),
        in_specs=[pl.BlockSpec((TM_IN, D), lambda i, j: (i, 0)),
                  pl.BlockSpec((1, 6, D), lambda i, j: (i // tiles_per_seq, 0, 0)),
                  pl.BlockSpec((1, D), lambda i, j: (0, 0)),
                  pl.BlockSpec((D, TN_IN), lambda i, j: (0, j)),
                  pl.BlockSpec((D, LANES), lambda i, j: (0, 0)),
                  pl.BlockSpec((SUBLANES, D), lambda i, j: (0, 0))],
        out_specs=[pl.BlockSpec((TM_IN, TN_IN), lambda i, j: (i, j)),
                   pl.BlockSpec((TM_IN, LANES), lambda i, j: (i, 0)),
                   pl.BlockSpec((SUBLANES, TM_IN), lambda i, j: (0, i))],
        out_shape=[jax.ShapeDtypeStruct((n, w_main.shape[1]), BF16),
                   jax.ShapeDtypeStruct((n, LANES), F32),
                   jax.ShapeDtypeStruct((SUBLANES, n), F32)],
        scratch_shapes=[pltpu.VMEM((TM_IN, D), BF16)],
        compiler_params=_params(("parallel", "arbitrary")),
        name="in_proj",
    )(x2d, mod_l, norm_w.reshape(1, D), w_main, wg_col, wg_row)


def _mlstm_kernel(mqk_ref, mv_ref, mo_ref, gcol_ref, grow_ref, cw_ref, cb_ref, shift_ref,
                  brow_ref, bcol_ref, nw_ref, o_ref, tail_scr, c_scr, m_scr):
    L = L_CHUNK

    @pl.when(pl.program_id(1) == 0)
    def _():
        tail_scr[...] = jnp.zeros_like(tail_scr)
        c_scr[...] = jnp.zeros_like(c_scr)
        m_scr[...] = jnp.zeros_like(m_scr)

    cur_b = mqk_ref[...]
    cur = cur_b.astype(F32)
    tail = tail_scr[...]
    row8 = lax.broadcasted_iota(jnp.int32, (SUBLANES, 1), 0)
    conv = cb_ref[...] + cur * cw_ref[CONV_K - 1:CONV_K, :]
    head = jnp.zeros((SUBLANES, D), F32)
    for d in range(1, CONV_K):
        w_d = cw_ref[CONV_K - 1 - d:CONV_K - d, :]
        conv = conv + jnp.dot(shift_ref[d - 1], cur_b, preferred_element_type=F32) * w_d
        head = head + jnp.where(row8 < d, pltpu.roll(tail, d, axis=0), 0.0) * w_d
    conv = jnp.concatenate([conv[:SUBLANES] + head, conv[SUBLANES:]], axis=0)
    tail_scr[...] = cur[L - SUBLANES:, :]
    qk = conv * _sigmoid(conv)

    gcol = gcol_ref[...] + brow_ref[...]
    grow = grow_ref[...] + bcol_ref[...]
    lf_col_all = -_softplus(-gcol)
    lf_row_all = -_softplus(-grow)

    r_idx = lax.broadcasted_iota(jnp.int32, (L, L), 0)
    c_idx = lax.broadcasted_iota(jnp.int32, (L, L), 1)
    causal = c_idx <= r_idx
    one_col = jnp.where(lax.broadcasted_iota(jnp.int32, (L, LANES), 1) == 0, 1.0, 0.0).astype(BF16)

    for h in range(N_MHEADS):
        ig_row = grow[h:h + 1, :]
        lf_row = lf_row_all[N_MHEADS + h:N_MHEADS + h + 1, :]
        ig_col = gcol[:, h:h + 1]
        lf_col = lf_col_all[:, N_MHEADS + h:N_MHEADS + h + 1]
        m_prev = m_scr[h:h + 1, 0:1]

        b_row = jnp.sum(jnp.where(c_idx >= r_idx, lf_col, 0.0), axis=0, keepdims=True)
        b_col = jnp.sum(jnp.where(causal, lf_row, 0.0), axis=1, keepdims=True)
        u_row = ig_row - b_row
        u_col = ig_col - b_col
        cm_col = jnp.maximum(m_prev, jnp.max(jnp.where(causal, u_row, NEG_INF), axis=1, keepdims=True))
        cm_last = jnp.maximum(m_prev, jnp.max(u_row, axis=1, keepdims=True))
        a_tot = jnp.sum(lf_row, axis=1, keepdims=True)

        w_intra = jnp.where(causal, jnp.exp(u_row - cm_col), 0.0)
        w_inter = jnp.exp(m_prev - cm_col)
        clamp = jnp.exp(-(b_col + cm_col))
        wc_col = jnp.exp(u_col - cm_last)
        decay = jnp.exp(m_prev - cm_last)

        q_h = (qk[:, h * DK:(h + 1) * DK] * (DK ** -0.5)).astype(BF16)
        k_f = qk[:, N_MHEADS * DK + h * DK:N_MHEADS * DK + (h + 1) * DK]
        k_h = k_f.astype(BF16)
        v_aug = jnp.concatenate([mv_ref[:, h * DV:(h + 1) * DV], one_col], axis=1)

        s = lax.dot_general(q_h, k_h, (((1,), (1,)), ((), ())), preferred_element_type=F32)
        s = (s * w_intra).astype(BF16)
        cn = c_scr[h]
        nd = jnp.dot(s, v_aug, preferred_element_type=F32)
        nd = nd + w_inter * jnp.dot(q_h, cn.astype(BF16), preferred_element_type=F32)
        num = nd[:, :DV]
        den = nd[:, DV:DV + 1]
        hh = num / jnp.maximum(jnp.abs(den), clamp)

        kw_t = (k_f * wc_col).T.astype(BF16)
        c_scr[h] = decay * cn + jnp.dot(kw_t, v_aug, preferred_element_type=F32)
        m_scr[h:h + 1, :] = jnp.broadcast_to(a_tot + cm_last, (1, LANES))

        hn = hh * lax.rsqrt(jnp.mean(hh * hh, axis=-1, keepdims=True) + EPS)
        hn = hn * nw_ref[:, h * DV:(h + 1) * DV]
        og = _sigmoid(mo_ref[:, h * DV:(h + 1) * DV].astype(F32))
        o_ref[:, h * DV:(h + 1) * DV] = (hn * og).astype(BF16)


def _mlstm(p, gcol, grow, conv_w, conv_b, brow, bcol, m_norm_w, bsz, seq):
    n = p.shape[0]
    nc = seq // L_CHUNK
    row = lambda b, c: b * nc + c
    t_idx = lax.broadcasted_iota(jnp.int32, (CONV_K - 1, L_CHUNK, L_CHUNK), 1)
    s_idx = lax.broadcasted_iota(jnp.int32, (CONV_K - 1, L_CHUNK, L_CHUNK), 2)
    d_idx = lax.broadcasted_iota(jnp.int32, (CONV_K - 1, L_CHUNK, L_CHUNK), 0) + 1
    shifts = jnp.where(t_idx - s_idx == d_idx, 1.0, 0.0).astype(BF16)
    return pl.pallas_call(
        _mlstm_kernel,
        grid=(bsz, nc),
        in_specs=[pl.BlockSpec((L_CHUNK, D), lambda b, c: (row(b, c), P_MQK)),
                  pl.BlockSpec((L_CHUNK, D), lambda b, c: (row(b, c), P_MV)),
                  pl.BlockSpec((L_CHUNK, D), lambda b, c: (row(b, c), P_MO)),
                  pl.BlockSpec((L_CHUNK, LANES), lambda b, c: (row(b, c), 0)),
                  pl.BlockSpec((SUBLANES, L_CHUNK), lambda b, c: (0, row(b, c))),
                  pl.BlockSpec((CONV_K, D), lambda b, c: (0, 0)),
                  pl.BlockSpec((1, D), lambda b, c: (0, 0)),
                  pl.BlockSpec((CONV_K - 1, L_CHUNK, L_CHUNK), lambda b, c: (0, 0, 0)),
                  pl.BlockSpec((1, LANES), lambda b, c: (0, 0)),
                  pl.BlockSpec((SUBLANES, 1), lambda b, c: (0, 0)),
                  pl.BlockSpec((1, D), lambda b, c: (0, 0))],
        out_specs=pl.BlockSpec((L_CHUNK, D), lambda b, c: (row(b, c), 0)),
        out_shape=jax.ShapeDtypeStruct((n, D), BF16),
        scratch_shapes=[pltpu.VMEM((SUBLANES, D), F32),
                        pltpu.VMEM((N_MHEADS, DK, DV + LANES), F32),
                        pltpu.VMEM((SUBLANES, LANES), F32)],
        compiler_params=_params(("parallel", "arbitrary")),
        name="mlstm",
    )(p, p, p, gcol, grow, conv_w, conv_b.reshape(1, D), shifts, brow, bcol, m_norm_w.reshape(1, D))


def _sb_cum(sp_blocks, tri):
    rows = []
    for sp in sp_blocks:
        rows.append(sp.astype(BF16))
    lhs = rows[0] if len(rows) == 1 else jnp.concatenate(rows, axis=0)
    cum = jnp.dot(lhs, tri, preferred_element_type=F32)
    return [cum[r * KB:(r + 1) * KB] for r in range(len(sp_blocks))]


def _nt_dot(a, b):
    return lax.dot_general(a, b, (((1,), (1,)), ((), ())), preferred_element_type=F32)


def _sb_tail(qm2, accs, carries, n_rem, k_ref, v_ref, tri):
    def live(c0, c1):
        return (jnp.max(jnp.maximum(c0, c1)) > SKIP_LOG).astype(jnp.int32)

    def cond(st):
        return jnp.logical_and(st[0] < n_rem, st[1] > 0)

    def body(st):
        j, _, a0, a1, c0, c1 = st
        k0 = pl.multiple_of((n_rem - 1 - j) * KB, KB)
        k_blk = k_ref[pl.ds(k0, KB), :]
        v_blk = v_ref[pl.ds(k0, KB), :]
        z_b = [_nt_dot(qm, k_blk) for qm in qm2]
        cum_b = _sb_cum([_softplus(z) for z in z_b], tri)
        acc_b, car_b = [a0, a1], [c0, c1]
        for hd in range(2):
            att = jnp.exp(z_b[hd] + cum_b[hd][:, :KB] + car_b[hd]).astype(BF16)
            acc_b[hd] = acc_b[hd] + jnp.dot(att, v_blk, preferred_element_type=F32)
            car_b[hd] = car_b[hd] + cum_b[hd][:, KB:]
        return (j + 1, live(car_b[0], car_b[1]), acc_b[0], acc_b[1], car_b[0], car_b[1])

    st = lax.while_loop(cond, body, (jnp.int32(0), live(carries[0], carries[1]),
                                     accs[0], accs[1], carries[0], carries[1]))
    return st[2], st[3]


def _sb_kernel(q_ref, k_ref, v_ref, tri_ref, o_ref):
    i = pl.program_id(2)
    tri = tri_ref[...]
    lane = lax.broadcasted_iota(jnp.int32, (1, LANES), 1)
    q2 = q_ref[...] * jnp.asarray(SB_DH ** -0.5, BF16)
    qms = [jnp.where((lane >= hd * SB_DH) & (lane < (hd + 1) * SB_DH), q2, jnp.zeros_like(q2))
           for hd in range(2)]
    nsub = TQ // KB
    nb = WIN // KB
    diff = (lax.broadcasted_iota(jnp.int32, (KB, WIN), 1)
            - lax.broadcasted_iota(jnp.int32, (KB, WIN), 0))

    zs, sps, valids, v_wins, n_rems = [], [], [], [], []
    for s in range(nsub):
        qs = i * TQ + s * KB
        ws = pl.multiple_of(jnp.maximum(qs - (WIN - KB), 0), KB)
        valid = diff < (qs - ws)
        k_win = k_ref[pl.ds(ws, WIN), :]
        for hd in range(2):
            z = _nt_dot(qms[hd][s * KB:(s + 1) * KB], k_win)
            zs.append(z)
            sps.append(jnp.where(valid, _softplus(z), 0.0))
        valids.append(valid)
        v_wins.append(v_ref[pl.ds(ws, WIN), :])
        n_rems.append(ws // KB)
    cums = _sb_cum([sp[:, kb * KB:(kb + 1) * KB] for sp in sps for kb in range(nb)], tri)

    accs, carries = [], []
    for u in range(2 * nsub):
        carry = jnp.zeros((KB, KB), F32)
        logits = [None] * nb
        for kb in reversed(range(nb)):
            cum = cums[u * nb + kb]
            logits[kb] = zs[u][:, kb * KB:(kb + 1) * KB] + cum[:, :KB] + carry
            carry = carry + cum[:, KB:]
        att = jnp.where(valids[u // 2], jnp.exp(jnp.concatenate(logits, axis=1)), 0.0).astype(BF16)
        accs.append(jnp.dot(att, v_wins[u // 2], preferred_element_type=F32))
        carries.append(carry)

    worst = functools.reduce(jnp.maximum, carries)
    need_tail = jnp.logical_and(jnp.max(worst) > SKIP_LOG, n_rems[-1] > 0)

    def run_tails():
        res = []
        for s in range(nsub):
            qm2 = [qms[hd][s * KB:(s + 1) * KB] for hd in range(2)]
            res.extend(_sb_tail(qm2, accs[2 * s:2 * s + 2], carries[2 * s:2 * s + 2],
                                n_rems[s], k_ref, v_ref, tri))
        return tuple(res)

    accs = lax.cond(need_tail, run_tails, lambda: tuple(accs))
    o_ref[...] = jnp.concatenate(
        [jnp.where(lane < SB_DH, accs[2 * s], accs[2 * s + 1]).astype(BF16) for s in range(nsub)], axis=0)


def _sb_attention(p, bsz, seq):
    n = p.shape[0]
    nq = seq // TQ
    pairs = SB_HEADS * SB_DH // LANES
    per_blk = D // LANES
    rr = lax.broadcasted_iota(jnp.int32, (KB, 2 * KB), 0)
    cc = lax.broadcasted_iota(jnp.int32, (KB, 2 * KB), 1)
    tri = jnp.where((cc >= KB) | (rr >= cc), -1.0, 0.0).astype(BF16)
    return pl.pallas_call(
        _sb_kernel,
        grid=(bsz, pairs, nq),
        in_specs=[pl.BlockSpec((TQ, LANES), lambda b, g, i: (b * nq + i, P_SQ * per_blk + g)),
                  pl.BlockSpec((seq, LANES), lambda b, g, i: (b, P_SK * per_blk + g)),
                  pl.BlockSpec((seq, LANES), lambda b, g, i: (b, P_SV * per_blk + g)),
                  pl.BlockSpec((KB, 2 * KB), lambda b, g, i: (0, 0))],
        out_specs=pl.BlockSpec((TQ, LANES), lambda b, g, i: (b * nq + i, g)),
        out_shape=jax.ShapeDtypeStruct((n, D), BF16),
        compiler_params=_params(("parallel", "parallel", "arbitrary")),
        name="sb_attention",
    )(p, p, p, tri)


def _route(h2, rwt_ref, rb_ref):
    tm = h2.shape[0]
    logits = lax.dot_general(rwt_ref[...], h2, (((1,), (1,)), ((), ())),
                             preferred_element_type=F32, precision=lax.Precision.HIGHEST)
    scores = _sigmoid(logits)
    sel = scores + rb_ref[...]
    sub = lax.broadcasted_iota(jnp.int32, (GROUP_SIZE, tm), 0).astype(F32)
    grp_scores = []
    for g in range(N_GROUPS):
        x = sel[g * GROUP_SIZE:(g + 1) * GROUP_SIZE, :]
        m1 = jnp.max(x, axis=0, keepdims=True)
        first = jnp.min(jnp.where(x == m1, sub, float(GROUP_SIZE)), axis=0, keepdims=True)
        m2 = jnp.max(jnp.where(sub == first, NEG_INF, x), axis=0, keepdims=True)
        grp_scores.append(m1 + m2)
    gs = jnp.concatenate(grp_scores, axis=0)
    g_iota = lax.broadcasted_iota(jnp.int32, (N_GROUPS, tm), 0)
    g_rank = jnp.zeros((N_GROUPS, tm), F32)
    for g in range(N_GROUPS):
        row = gs[g:g + 1, :]
        g_rank = g_rank + jnp.where(row > gs, 1.0, jnp.where(row == gs, jnp.where(g_iota > g, 1.0, 0.0), 0.0))
    g_sel = g_rank < TOPK_GROUPS
    masked = jnp.concatenate(
        [jnp.where(g_sel[g:g + 1, :], sel[g * GROUP_SIZE:(g + 1) * GROUP_SIZE, :], NEG_INF)
         for g in range(N_GROUPS)], axis=0)
    e_iota = lax.broadcasted_iota(jnp.int32, (N_EXPERTS, tm), 0)
    e_rank = jnp.zeros((N_EXPERTS, tm), F32)
    for e in range(N_EXPERTS):
        row = masked[e:e + 1, :]
        e_rank = e_rank + jnp.where(row > masked, 1.0,
                                    jnp.where(row == masked, jnp.where(e_iota > e, 1.0, 0.0), 0.0))
    w_sel = jnp.where(e_rank < TOP_K, scores, 0.0)
    gates_t = w_sel / jnp.sum(w_sel, axis=0, keepdims=True) * ROUTED_SCALE
    gates_t = jnp.concatenate([gates_t, jnp.zeros((LANES - N_EXPERTS, tm), F32)], axis=0)
    return gates_t.T


def _mix_kernel(hm_ref, hs_ref, ga_ref, gb_ref, x_ref, mod_ref, wa_ref, wb_ref, wo_ref,
                nw_ref, rwt_ref, rb_ref, x1_ref, h2_ref, gates_ref):
    ya = jnp.dot(hm_ref[...], wa_ref[...], preferred_element_type=F32)
    yb = jnp.dot(hs_ref[...], wb_ref[...], preferred_element_type=F32)
    y = _sigmoid(ga_ref[...].astype(F32)) * ya + _sigmoid(gb_ref[...].astype(F32)) * yb
    mix = jnp.dot(y.astype(BF16), wo_ref[...], preferred_element_type=F32)
    m = mod_ref[0]
    x1 = x_ref[...] + m[2:3] * mix
    x1_ref[...] = x1
    ms = jnp.mean(x1 * x1, axis=-1, keepdims=True)
    h2 = x1 * lax.rsqrt(ms + EPS) * nw_ref[...]
    h2 = h2 * (1.0 + m[4:5]) + m[3:4]
    h2_ref[...] = h2.astype(BF16)
    gates_ref[...] = _route(h2, rwt_ref, rb_ref)


def _mix(hm, hs, p, x2d, mod_l, wa, wb, wo, norm2_w, rwt, rb, seq):
    n = x2d.shape[0]
    tiles_per_seq = seq // TM_MIX
    tile = lambda i: (i, 0)
    const = lambda i: (0, 0)
    return pl.pallas_call(
        _mix_kernel,
        grid=(n // TM_MIX,),
        in_specs=[pl.BlockSpec((TM_MIX, D), tile),
                  pl.BlockSpec((TM_MIX, D), tile),
                  pl.BlockSpec((TM_MIX, D), lambda i: (i, P_GA)),
                  pl.BlockSpec((TM_MIX, D), lambda i: (i, P_GB)),
                  pl.BlockSpec((TM_MIX, D), tile),
                  pl.BlockSpec((1, 6, D), lambda i: (i // tiles_per_seq, 0, 0)),
                  pl.BlockSpec((D, D), const),
                  pl.BlockSpec((D, D), const),
                  pl.BlockSpec((D, D), const),
                  pl.BlockSpec((1, D), const),
                  pl.BlockSpec((N_EXPERTS, D), const),
                  pl.BlockSpec((N_EXPERTS, 1), const)],
        out_specs=[pl.BlockSpec((TM_MIX, D), tile),
                   pl.BlockSpec((TM_MIX, D), tile),
                   pl.BlockSpec((TM_MIX, LANES), tile)],
        out_shape=[jax.ShapeDtypeStruct((n, D), F32),
                   jax.ShapeDtypeStruct((n, D), BF16),
                   jax.ShapeDtypeStruct((n, LANES), F32)],
        compiler_params=_params(("parallel",)),
        name="mix_out",
    )(hm, hs, p, p, x2d, mod_l, wa, wb, wo, norm2_w.reshape(1, D), rwt, rb)


def _moe_kernel(h2_ref, gates_ref, x1_ref, mod_ref, wg_ref, wu_ref, wd_ref,
                sg_ref, su_ref, sd_ref, fnw_ref, o_ref, wg_b, wu_b, wd_b, *, final_norm):
    e = pl.program_id(1)
    chunks = [pl.ds(r * R_MOE, R_MOE) for r in range(TM_MOE // R_MOE)]

    def swiglu(rows, gate_cols):
        xr = h2_ref[rows, :]
        g = jnp.dot(xr, wg_b[...], preferred_element_type=F32)
        u = jnp.dot(xr, wu_b[...], preferred_element_type=F32)
        hid = g * _sigmoid(g) * u
        if gate_cols is not None:
            hid = hid * gate_cols
        return jnp.dot(hid.astype(BF16), wd_b[...], preferred_element_type=F32)

    @pl.when(e == 0)
    def _():
        wg_b[...] = sg_ref[0].astype(BF16)
        wu_b[...] = su_ref[0].astype(BF16)
        wd_b[...] = sd_ref[0].astype(BF16)
        for rows in chunks:
            o_ref[rows, :] = swiglu(rows, None)

    wg_b[:, :EXPERT_FF] = wg_ref[0, 0].astype(BF16)
    wg_b[:, EXPERT_FF:] = wg_ref[0, 1].astype(BF16)
    wu_b[:, :EXPERT_FF] = wu_ref[0, 0].astype(BF16)
    wu_b[:, EXPERT_FF:] = wu_ref[0, 1].astype(BF16)
    wd_b[:EXPERT_FF, :] = wd_ref[0, 0].astype(BF16)
    wd_b[EXPERT_FF:, :] = wd_ref[0, 1].astype(BF16)
    lane = lax.broadcasted_iota(jnp.int32, (1, LANES), 1)
    first_half = lax.broadcasted_iota(jnp.int32, (1, 2 * EXPERT_FF), 1) < EXPERT_FF
    for rows in chunks:
        gts = gates_ref[rows, :]
        g0 = jnp.sum(jnp.where(lane == 2 * e, gts, 0.0), axis=1, keepdims=True)
        g1 = jnp.sum(jnp.where(lane == 2 * e + 1, gts, 0.0), axis=1, keepdims=True)
        o_ref[rows, :] += swiglu(rows, jnp.where(first_half, g0, g1))

    @pl.when(e == pl.num_programs(1) - 1)
    def _():
        x2 = x1_ref[...] + mod_ref[0][5:6] * o_ref[...]
        if final_norm:
            ms = jnp.mean(x2 * x2, axis=-1, keepdims=True)
            x2 = x2 * lax.rsqrt(ms + EPS) * fnw_ref[...]
        o_ref[...] = x2


def _moe(h2, gates, x1, mod_l, layer, w_gate, w_up, w_down, sh_gate, sh_up, sh_down, fnw, seq, final_norm):
    n = h2.shape[0]
    tiles_per_seq = seq // TM_MOE
    tile = lambda i, e: (i, 0)
    once = pl.Buffered(1)
    return pl.pallas_call(
        functools.partial(_moe_kernel, final_norm=final_norm),
        grid=(n // TM_MOE, N_EXPERTS // 2),
        in_specs=[pl.BlockSpec((TM_MOE, D), tile, pipeline_mode=once),
                  pl.BlockSpec((TM_MOE, LANES), tile, pipeline_mode=once),
                  pl.BlockSpec((TM_MOE, D), tile, pipeline_mode=once),
                  pl.BlockSpec((1, 6, D), lambda i, e: (i // tiles_per_seq, 0, 0)),
                  pl.BlockSpec((1, 2, D, EXPERT_FF), lambda i, e: (layer, e, 0, 0)),
                  pl.BlockSpec((1, 2, D, EXPERT_FF), lambda i, e: (layer, e, 0, 0)),
                  pl.BlockSpec((1, 2, EXPERT_FF, D), lambda i, e: (layer, e, 0, 0)),
                  pl.BlockSpec((1, D, SHARED_FF), lambda i, e: (layer, 0, 0), pipeline_mode=once),
                  pl.BlockSpec((1, D, SHARED_FF), lambda i, e: (layer, 0, 0), pipeline_mode=once),
                  pl.BlockSpec((1, SHARED_FF, D), lambda i, e: (layer, 0, 0), pipeline_mode=once),
                  pl.BlockSpec((1, D), lambda i, e: (0, 0))],
        out_specs=pl.BlockSpec((TM_MOE, D), tile),
        out_shape=jax.ShapeDtypeStruct((n, D), F32),
        scratch_shapes=[pltpu.VMEM((D, 2 * EXPERT_FF), BF16),
                        pltpu.VMEM((D, 2 * EXPERT_FF), BF16),
                        pltpu.VMEM((2 * EXPERT_FF, D), BF16)],
        compiler_params=_params(("parallel", "arbitrary")),
        name="moe",
    )(h2, gates, x1, mod_l, w_gate, w_up, w_down, sh_gate, sh_up, sh_down, fnw.reshape(1, D))


def kernel(x, c, ada_w, ada_b, norm1_w, w_in, conv_w, conv_b, m_igate_b, m_fgate_b, m_norm_w, w_proj_a, w_proj_b, w_out, norm2_w, router_w, router_b, w_gate, w_up, w_down, sh_gate, sh_up, sh_down, final_norm_w):
    bsz, seq, _ = x.shape
    n = bsz * seq
    depth = ada_w.shape[0]
    assert 2 * N_MHEADS == SUBLANES and SHARED_FF == 2 * EXPERT_FF and seq % TM_IN == 0 and seq % L_CHUNK == 0 and seq % TQ == 0 and seq >= WIN and WIN % KB == 0 and TQ % KB == 0

    c_pad = jnp.zeros((SUBLANES, D), F32).at[:bsz].set(c)
    mod = _ada(c_pad, ada_w, ada_b)[:, :bsz].reshape(depth, bsz, 6, D)

    gate_lo = 2 * N_MHEADS * DK + 2 * N_MHEADS * DV
    gate_hi = gate_lo + 2 * N_MHEADS
    xc = x.reshape(n, D)
    for l in range(depth):
        w_main, wg_col, wg_row = _repack_w_in(w_in, l, gate_lo, gate_hi - gate_lo)
        gate_b = jnp.concatenate([m_igate_b[l], m_fgate_b[l]]).astype(F32)
        brow = jnp.zeros((1, LANES), F32).at[0, :2 * N_MHEADS].set(gate_b)

        p, gcol, grow = _inproj(xc, mod[l], norm1_w[l], w_main, wg_col, wg_row, seq)
        hm = _mlstm(p, gcol, grow, conv_w[l], conv_b[l], brow, gate_b.reshape(2 * N_MHEADS, 1),
                    m_norm_w[l], bsz, seq)
        hs = _sb_attention(p, bsz, seq)
        x1, h2, gates = _mix(hm, hs, p, xc, mod[l], w_proj_a[l].astype(BF16), w_proj_b[l].astype(BF16),
                             w_out[l].astype(BF16), norm2_w[l], router_w[l].T,
                             router_b[l].reshape(N_EXPERTS, 1), seq)
        xc = _moe(h2, gates, x1, mod[l], l, w_gate, w_up, w_down, sh_gate, sh_up, sh_down,
                  final_norm_w, seq, final_norm=(l == depth - 1))
    return xc.reshape(bsz, seq, D)
```

```python
import functools

import jax
import jax.numpy as jnp
from jax import lax
from jax.experimental import pallas as pl
from jax.experimental.pallas import tpu as pltpu

F32 = jnp.float32
BF16 = jnp.bfloat16

D = 1024
DEPTH = 2
N_MHEADS = 4
DK = 128
DV = 256
CONV_K = 4
SB_HEADS = 16
SB_DH = 64
N_EXPERTS = 64
TOP_K = 8
N_GROUPS = 8
TOPK_GROUPS = 4
GROUP_SIZE = N_EXPERTS // N_GROUPS
EXPERT_FF = 128
SHARED_FF = 256
ROUTED_SCALE = 2.5
EPS = 1e-6

LANES = 128
SUBLANES = 8
VMEM_LIMIT = 56 * 1024 * 1024

P_MQK, P_MV, P_MO, P_SQ, P_SK, P_SV, P_GA, P_GB = range(8)

TM_IN = 2048
TN_IN = 1024
L_CHUNK = 256
TQ = 1024
AV_LAG = 4
KB = 128
WIN = 384
SKIP_LOG = -104.0
TM_MIX = 512
TM_MOE = 2048
R_MOE = 1024
NEG_INF = float("-inf")
LOG2E = 1.4426950408889634


def _sigmoid(x):
    return 0.5 * jnp.tanh(0.5 * x) + 0.5


def _softplus(x):
    return jnp.maximum(x, 0.0) + jnp.log(1.0 + jnp.exp2(jnp.abs(x) * (-LOG2E)))


def _params(sem):
    return pltpu.CompilerParams(dimension_semantics=sem, vmem_limit_bytes=VMEM_LIMIT)


def _ada_kernel(c_ref, w_ref, b_ref, o_ref):
    c = c_ref[...]
    cond = c * _sigmoid(c)
    o_ref[0] = jnp.dot(cond, w_ref[0], preferred_element_type=F32,
                       precision=lax.Precision.HIGHEST) + b_ref[0]


def _ada(c_pad, ada_w, ada_b):
    tn = 1536
    depth = ada_w.shape[0]
    return pl.pallas_call(
        _ada_kernel,
        grid=(depth, 6 * D // tn),
        in_specs=[pl.BlockSpec((SUBLANES, D), lambda l, j: (0, 0)),
                  pl.BlockSpec((1, D, tn), lambda l, j: (l, 0, j)),
                  pl.BlockSpec((1, 1, tn), lambda l, j: (l, 0, j))],
        out_specs=pl.BlockSpec((1, SUBLANES, tn), lambda l, j: (l, 0, j)),
        out_shape=jax.ShapeDtypeStruct((depth, SUBLANES, 6 * D), F32),
        compiler_params=_params(("parallel", "parallel")),
        name="ada_mod",
    )(c_pad, ada_w, ada_b.reshape(depth, 1, 6 * D))


def _repack_kernel(w_ref, main_ref, gcol_ref, grow_ref, *, gate_lo, n_gate):
    w = w_ref[0]
    width = w.shape[1]
    main_ref[:, :gate_lo] = w[:, :gate_lo].astype(BF16)
    main_ref[:, gate_lo:] = w[:, gate_lo + n_gate:width].astype(BF16)
    lane = lax.broadcasted_iota(jnp.int32, (1, LANES), 1)
    g = jnp.where(lane < n_gate, w[:, gate_lo:gate_lo + LANES], 0.0)
    gcol_ref[...] = g.astype(BF16)
    grow_ref[...] = g.T[:SUBLANES, :].astype(BF16)


def _repack_w_in(w_in, layer, gate_lo, n_gate):
    depth, d, width = w_in.shape
    tr = 256
    return pl.pallas_call(
        functools.partial(_repack_kernel, gate_lo=gate_lo, n_gate=n_gate),
        grid=(d // tr,),
        in_specs=[pl.BlockSpec((1, tr, width), lambda i: (layer, i, 0))],
        out_specs=[pl.BlockSpec((tr, width - n_gate), lambda i: (i, 0)),
                   pl.BlockSpec((tr, LANES), lambda i: (i, 0)),
                   pl.BlockSpec((SUBLANES, tr), lambda i: (0, i))],
        out_shape=[jax.ShapeDtypeStruct((d, width - n_gate), BF16),
                   jax.ShapeDtypeStruct((d, LANES), BF16),
                   jax.ShapeDtypeStruct((SUBLANES, d), BF16)],
        compiler_params=_params(("parallel",)),
        name="repack_w_in",
    )(w_in)


def _inproj_kernel(x_ref, mod_ref, nw_ref, w_ref, wgc_ref, wgr_ref,
                   p_ref, gcol_ref, grow_ref, h_scr):
    @pl.when(pl.program_id(1) == 0)
    def _():
        x = x_ref[...]
        ms = jnp.mean(x * x, axis=-1, keepdims=True)
        y = x * lax.rsqrt(ms + EPS) * nw_ref[...]
        m = mod_ref[0]
        hb = (y * (1.0 + m[1:2]) + m[0:1]).astype(BF16)
        h_scr[...] = hb
        gcol_ref[...] = jnp.dot(hb, wgc_ref[...], preferred_element_type=F32)
        grow_ref[...] = lax.dot_general(wgr_ref[...], hb, (((1,), (1,)), ((), ())),
                                        preferred_element_type=F32)

    p_ref[...] = jnp.dot(h_scr[...], w_ref[...], preferred_element_type=F32).astype(BF16)


def _inproj(x2d, mod_l, norm_w, w_main, wg_col, wg_row, seq):
    n = x2d.shape[0]
    tiles_per_seq = seq // TM_IN
    return pl.pallas_call(
        _inproj_kernel,
        grid=(n // TM_IN, w_main.shape[1] // TN_IN),
        in_specs=[pl.BlockSpec((TM_IN, D), lambda i, j: (i, 0)),
                  pl.BlockSpec((1, 6, D), lambda i, j: (i // tiles_per_seq, 0, 0)),
                  pl.BlockSpec((1, D), lambda i, j: (0, 0)),
                  pl.BlockSpec((D, TN_IN), lambda i, j: (0, j)),
                  pl.BlockSpec((D, LANES), lambda i, j: (0, 0)),
                  pl.BlockSpec((SUBLANES, D), lambda i, j: (0, 0))],
        out_specs=[pl.BlockSpec((TM_IN, TN_IN), lambda i, j: (i, j)),
                   pl.BlockSpec((TM_IN, LANES), lambda i, j: (i, 0)),
                   pl.BlockSpec((SUBLANES, TM_IN), lambda i, j: (0, i))],
        out_shape=[jax.ShapeDtypeStruct((n, w_main.shape[1]), BF16),
                   jax.ShapeDtypeStruct((n, LANES), F32),
                   jax.ShapeDtypeStruct((SUBLANES, n), F32)],
        scratch_shapes=[pltpu.VMEM((TM_IN, D), BF16)],
        compiler_params=_params(("parallel", "arbitrary")),
        name="in_proj",
    )(x2d, mod_l, norm_w.reshape(1, D), w_main, wg_col, wg_row)


def _mlstm_kernel(mqk_ref, mv_ref, mo_ref, gcol_ref, grow_ref, cw_ref, cb_ref, shift_ref,
                  brow_ref, bcol_ref, nw_ref, o_ref, tail_scr, c_scr, m_scr):
    L = L_CHUNK

    @pl.when(pl.program_id(1) == 0)
    def _():
        tail_scr[...] = jnp.zeros_like(tail_scr)
        c_scr[...] = jnp.zeros_like(c_scr)
        m_scr[...] = jnp.zeros_like(m_scr)

    cur_b = mqk_ref[...]
    cur = cur_b.astype(F32)
    tail = tail_scr[...]
    row8 = lax.broadcasted_iota(jnp.int32, (SUBLANES, 1), 0)
    conv = cb_ref[...] + cur * cw_ref[CONV_K - 1:CONV_K, :]
    head = jnp.zeros((SUBLANES, D), F32)
    for d in range(1, CONV_K):
        w_d = cw_ref[CONV_K - 1 - d:CONV_K - d, :]
        conv = conv + jnp.dot(shift_ref[d - 1], cur_b, preferred_element_type=F32) * w_d
        head = head + jnp.where(row8 < d, pltpu.roll(tail, d, axis=0), 0.0) * w_d
    conv = jnp.concatenate([conv[:SUBLANES] + head, conv[SUBLANES:]], axis=0)
    tail_scr[...] = cur[L - SUBLANES:, :]
    qk = conv * _sigmoid(conv)

    gcol = gcol_ref[...] + brow_ref[...]
    grow = grow_ref[...] + bcol_ref[...]
    lf_col_all = -_softplus(-gcol)
    lf_row_all = -_softplus(-grow)

    r_idx = lax.broadcasted_iota(jnp.int32, (L, L), 0)
    c_idx = lax.broadcasted_iota(jnp.int32, (L, L), 1)
    causal = c_idx <= r_idx
    one_col = jnp.where(lax.broadcasted_iota(jnp.int32, (L, LANES), 1) == 0, 1.0, 0.0).astype(BF16)

    for h in range(N_MHEADS):
        ig_row = grow[h:h + 1, :]
        lf_row = lf_row_all[N_MHEADS + h:N_MHEADS + h + 1, :]
        ig_col = gcol[:, h:h + 1]
        lf_col = lf_col_all[:, N_MHEADS + h:N_MHEADS + h + 1]
        m_prev = m_scr[h:h + 1, 0:1]

        b_row = jnp.sum(jnp.where(c_idx >= r_idx, lf_col, 0.0), axis=0, keepdims=True)
        b_col = jnp.sum(jnp.where(causal, lf_row, 0.0), axis=1, keepdims=True)
        u_row = ig_row - b_row
        u_col = ig_col - b_col
        cm_col = jnp.maximum(m_prev, jnp.max(jnp.where(causal, u_row, NEG_INF), axis=1, keepdims=True))
        cm_last = jnp.maximum(m_prev, jnp.max(u_row, axis=1, keepdims=True))
        a_tot = jnp.sum(lf_row, axis=1, keepdims=True)

        w_intra = jnp.where(causal, jnp.exp(u_row - cm_col), 0.0)
        w_inter = jnp.exp(m_prev - cm_col)
        clamp = jnp.exp(-(b_col + cm_col))
        wc_col = jnp.exp(u_col - cm_last)
        decay = jnp.exp(m_prev - cm_last)

        q_h = (qk[:, h * DK:(h + 1) * DK] * (DK ** -0.5)).astype(BF16)
        k_f = qk[:, N_MHEADS * DK + h * DK:N_MHEADS * DK + (h + 1) * DK]
        k_h = k_f.astype(BF16)
        v_aug = jnp.concatenate([mv_ref[:, h * DV:(h + 1) * DV], one_col], axis=1)

        s = lax.dot_general(q_h, k_h, (((1,), (1,)), ((), ())), preferred_element_type=F32)
        s = (s * w_intra).astype(BF16)
        cn = c_scr[h]
        nd = jnp.dot(s, v_aug, preferred_element_type=F32)
        nd = nd + w_inter * jnp.dot(q_h, cn.astype(BF16), preferred_element_type=F32)
        num = nd[:, :DV]
        den = nd[:, DV:DV + 1]
        hh = num / jnp.maximum(jnp.abs(den), clamp)

        kw_t = (k_f * wc_col).T.astype(BF16)
        c_scr[h] = decay * cn + jnp.dot(kw_t, v_aug, preferred_element_type=F32)
        m_scr[h:h + 1, :] = jnp.broadcast_to(a_tot + cm_last, (1, LANES))

        hn = hh * lax.rsqrt(jnp.mean(hh * hh, axis=-1, keepdims=True) + EPS)
        hn = hn * nw_ref[:, h * DV:(h + 1) * DV]
        og = _sigmoid(mo_ref[:, h * DV:(h + 1) * DV].astype(F32))
        o_ref[:, h * DV:(h + 1) * DV] = (hn * og).astype(BF16)


def _mlstm(p, gcol, grow, conv_w, conv_b, brow, bcol, m_norm_w, bsz, seq):
    n = p.shape[0]
    nc = seq // L_CHUNK
    row = lambda b, c: b * nc + c
    t_idx = lax.broadcasted_iota(jnp.int32, (CONV_K - 1, L_CHUNK, L_CHUNK), 1)
    s_idx = lax.broadcasted_iota(jnp.int32, (CONV_K - 1, L_CHUNK, L_CHUNK), 2)
    d_idx = lax.broadcasted_iota(jnp.int32, (CONV_K - 1, L_CHUNK, L_CHUNK), 0) + 1
    shifts = jnp.where(t_idx - s_idx == d_idx, 1.0, 0.0).astype(BF16)
    return pl.pallas_call(
        _mlstm_kernel,
        grid=(bsz, nc),
        in_specs=[pl.BlockSpec((L_CHUNK, D), lambda b, c: (row(b, c), P_MQK)),
                  pl.BlockSpec((L_CHUNK, D), lambda b, c: (row(b, c), P_MV)),
                  pl.BlockSpec((L_CHUNK, D), lambda b, c: (row(b, c), P_MO)),
                  pl.BlockSpec((L_CHUNK, LANES), lambda b, c: (row(b, c), 0)),
                  pl.BlockSpec((SUBLANES, L_CHUNK), lambda b, c: (0, row(b, c))),
                  pl.BlockSpec((CONV_K, D), lambda b, c: (0, 0)),
                  pl.BlockSpec((1, D), lambda b, c: (0, 0)),
                  pl.BlockSpec((CONV_K - 1, L_CHUNK, L_CHUNK), lambda b, c: (0, 0, 0)),
                  pl.BlockSpec((1, LANES), lambda b, c: (0, 0)),
                  pl.BlockSpec((SUBLANES, 1), lambda b, c: (0, 0)),
                  pl.BlockSpec((1, D), lambda b, c: (0, 0))],
        out_specs=pl.BlockSpec((L_CHUNK, D), lambda b, c: (row(b, c), 0)),
        out_shape=jax.ShapeDtypeStruct((n, D), BF16),
        scratch_shapes=[pltpu.VMEM((SUBLANES, D), F32),
                        pltpu.VMEM((N_MHEADS, DK, DV + LANES), F32),
                        pltpu.VMEM((SUBLANES, LANES), F32)],
        compiler_params=_params(("parallel", "arbitrary")),
        name="mlstm",
    )(p, p, p, gcol, grow, conv_w, conv_b.reshape(1, D), shifts, brow, bcol, m_norm_w.reshape(1, D))


def _sb_cum(sp_blocks, tri):
    rows = []
    for sp in sp_blocks:
        rows.append(sp.astype(BF16))
    lhs = rows[0] if len(rows) == 1 else jnp.concatenate(rows, axis=0)
    cum = jnp.dot(lhs, tri, preferred_element_type=F32)
    return [cum[r * KB:(r + 1) * KB] for r in range(len(sp_blocks))]


def _nt_dot(a, b):
    return lax.dot_general(a, b, (((1,), (1,)), ((), ())), preferred_element_type=F32)


def _sb_tail(qm2, accs, carries, n_rem, k_ref, v_ref, tri):
    def live(c0, c1):
        return (jnp.max(jnp.maximum(c0, c1)) > SKIP_LOG).astype(jnp.int32)

    def cond(st):
        return jnp.logical_and(st[0] < n_rem, st[1] > 0)

    def body(st):
        j, _, a0, a1, c0, c1 = st
        k0 = pl.multiple_of((n_rem - 1 - j) * KB, KB)
        k_blk = k_ref[pl.ds(k0, KB), :]
        v_blk = v_ref[pl.ds(k0, KB), :]
        z_b = [_nt_dot(qm, k_blk) for qm in qm2]
        cum_b = _sb_cum([_softplus(z) for z in z_b], tri)
        acc_b, car_b = [a0, a1], [c0, c1]
        for hd in range(2):
            att = jnp.exp(z_b[hd] + cum_b[hd][:, :KB] + car_b[hd]).astype(BF16)
            acc_b[hd] = acc_b[hd] + jnp.dot(att, v_blk, preferred_element_type=F32)
            car_b[hd] = car_b[hd] + cum_b[hd][:, KB:]
        return (j + 1, live(car_b[0], car_b[1]), acc_b[0], acc_b[1], car_b[0], car_b[1])

    st = lax.while_loop(cond, body, (jnp.int32(0), live(carries[0], carries[1]),
                                     accs[0], accs[1], carries[0], carries[1]))
    return st[2], st[3]


def _sb_kernel(q_ref, k_ref, v_ref, tri_ref, o_ref):
    i = pl.program_id(2)
    tri = tri_ref[...]
    lane = lax.broadcasted_iota(jnp.int32, (1, LANES), 1)
    q2 = q_ref[...] * jnp.asarray(SB_DH ** -0.5, BF16)
    qms = [jnp.where((lane >= hd * SB_DH) & (lane < (hd + 1) * SB_DH), q2, jnp.zeros_like(q2))
           for hd in range(2)]
    nsub = TQ // KB
    nb = WIN // KB
    diff = (lax.broadcasted_iota(jnp.int32, (KB, WIN), 1)
            - lax.broadcasted_iota(jnp.int32, (KB, WIN), 0))
    below_diag = diff[:, :KB] < 0

    st = [dict() for _ in range(nsub)]

    def stage_scores(s):
        qs = i * TQ + s * KB
        ws = pl.multiple_of(jnp.maximum(qs - (WIN - KB), 0), KB)
        if s * KB >= WIN - KB:
            masks = [None] * (nb - 1) + [below_diag]
        else:
            valid = diff < (qs - ws)
            masks = [valid[:, kb * KB:(kb + 1) * KB] for kb in range(nb)]
        k_win = k_ref[pl.ds(ws, WIN), :]
        zs = [_nt_dot(qms[hd][s * KB:(s + 1) * KB], k_win) for hd in range(2)]
        sps = []
        for z in zs:
            for kb in range(nb):
                sp = _softplus(z[:, kb * KB:(kb + 1) * KB])
                sps.append(sp if masks[kb] is None else jnp.where(masks[kb], sp, 0.0))
        st[s].update(masks=masks, z=zs, cum=_sb_cum(sps, tri), v=v_ref[pl.ds(ws, WIN), :], n_rem=ws // KB)

    def stage_output(s):
        st[s]["acc"], st[s]["carry"] = [], []
        for hd in range(2):
            carry = jnp.zeros((KB, KB), F32)
            att = [None] * nb
            for kb in reversed(range(nb)):
                cum = st[s]["cum"][hd * nb + kb]
                w = jnp.exp(st[s]["z"][hd][:, kb * KB:(kb + 1) * KB] + cum[:, :KB] + carry)
                m = st[s]["masks"][kb]
                att[kb] = (w if m is None else jnp.where(m, w, 0.0)).astype(BF16)
                carry = carry + cum[:, KB:]
            st[s]["acc"].append(jnp.dot(jnp.concatenate(att, axis=1), st[s]["v"], preferred_element_type=F32))
            st[s]["carry"].append(carry)

    for t in range(nsub + AV_LAG):
        if t < nsub:
            stage_scores(t)
        if 0 <= t - AV_LAG < nsub:
            stage_output(t - AV_LAG)
    accs = [a for s in range(nsub) for a in st[s]["acc"]]
    carries = [c for s in range(nsub) for c in st[s]["carry"]]
    n_rems = [st[s]["n_rem"] for s in range(nsub)]

    worst = functools.reduce(jnp.maximum, carries)
    need_tail = jnp.logical_and(jnp.max(worst) > SKIP_LOG, n_rems[-1] > 0)

    def run_tails():
        res = []
        for s in range(nsub):
            qm2 = [qms[hd][s * KB:(s + 1) * KB] for hd in range(2)]
            res.extend(_sb_tail(qm2, accs[2 * s:2 * s + 2], carries[2 * s:2 * s + 2],
                                n_rems[s], k_ref, v_ref, tri))
        return tuple(res)

    accs = lax.cond(need_tail, run_tails, lambda: tuple(accs))
    o_ref[...] = jnp.concatenate(
        [jnp.where(lane < SB_DH, accs[2 * s], accs[2 * s + 1]).astype(BF16) for s in range(nsub)], axis=0)


def _sb_attention(p, bsz, seq):
    n = p.shape[0]
    nq = seq // TQ
    pairs = SB_HEADS * SB_DH // LANES
    per_blk = D // LANES
    rr = lax.broadcasted_iota(jnp.int32, (KB, 2 * KB), 0)
    cc = lax.broadcasted_iota(jnp.int32, (KB, 2 * KB), 1)
    tri = jnp.where((cc >= KB) | (rr >= cc), -1.0, 0.0).astype(BF16)
    return pl.pallas_call(
        _sb_kernel,
        grid=(bsz, pairs, nq),
        in_specs=[pl.BlockSpec((TQ, LANES), lambda b, g, i: (b * nq + i, P_SQ * per_blk + g)),
                  pl.BlockSpec((seq, LANES), lambda b, g, i: (b, P_SK * per_blk + g)),
                  pl.BlockSpec((seq, LANES), lambda b, g, i: (b, P_SV * per_blk + g)),
                  pl.BlockSpec((KB, 2 * KB), lambda b, g, i: (0, 0))],
        out_specs=pl.BlockSpec((TQ, LANES), lambda b, g, i: (b * nq + i, g)),
        out_shape=jax.ShapeDtypeStruct((n, D), BF16),
        compiler_params=_params(("parallel", "parallel", "arbitrary")),
        name="sb_attention",
    )(p, p, p, tri)


def _route(h2, rwt_ref, rb_ref):
    tm = h2.shape[0]
    logits = lax.dot_general(rwt_ref[...], h2, (((1,), (1,)), ((), ())),
                             preferred_element_type=F32, precision=lax.Precision.HIGHEST)
    scores = _sigmoid(logits)
    sel = scores + rb_ref[...]
    sub = lax.broadcasted_iota(jnp.int32, (GROUP_SIZE, tm), 0).astype(F32)
    grp_scores = []
    for g in range(N_GROUPS):
        x = sel[g * GROUP_SIZE:(g + 1) * GROUP_SIZE, :]
        m1 = jnp.max(x, axis=0, keepdims=True)
        first = jnp.min(jnp.where(x == m1, sub, float(GROUP_SIZE)), axis=0, keepdims=True)
        m2 = jnp.max(jnp.where(sub == first, NEG_INF, x), axis=0, keepdims=True)
        grp_scores.append(m1 + m2)
    gs = jnp.concatenate(grp_scores, axis=0)
    g_iota = lax.broadcasted_iota(jnp.int32, (N_GROUPS, tm), 0)
    g_rank = jnp.zeros((N_GROUPS, tm), F32)
    for g in range(N_GROUPS):
        row = gs[g:g + 1, :]
        g_rank = g_rank + jnp.where(row > gs, 1.0, jnp.where(row == gs, jnp.where(g_iota > g, 1.0, 0.0), 0.0))
    g_sel = g_rank < TOPK_GROUPS
    masked = jnp.concatenate(
        [jnp.where(g_sel[g:g + 1, :], sel[g * GROUP_SIZE:(g + 1) * GROUP_SIZE, :], NEG_INF)
         for g in range(N_GROUPS)], axis=0)
    e_iota = lax.broadcasted_iota(jnp.int32, (N_EXPERTS, tm), 0)
    e_rank = jnp.zeros((N_EXPERTS, tm), F32)
    for e in range(N_EXPERTS):
        row = masked[e:e + 1, :]
        e_rank = e_rank + jnp.where(row > masked, 1.0,
                                    jnp.where(row == masked, jnp.where(e_iota > e, 1.0, 0.0), 0.0))
    w_sel = jnp.where(e_rank < TOP_K, scores, 0.0)
    gates_t = w_sel / jnp.sum(w_sel, axis=0, keepdims=True) * ROUTED_SCALE
    gates_t = jnp.concatenate([gates_t, jnp.zeros((LANES - N_EXPERTS, tm), F32)], axis=0)
    return gates_t.T


def _mix_kernel(hm_ref, hs_ref, ga_ref, gb_ref, x_ref, mod_ref, wa_ref, wb_ref, wo_ref,
                nw_ref, rwt_ref, rb_ref, x1_ref, h2_ref, gates_ref):
    ya = jnp.dot(hm_ref[...], wa_ref[...], preferred_element_type=F32)
    yb = jnp.dot(hs_ref[...], wb_ref[...], preferred_element_type=F32)
    y = _sigmoid(ga_ref[...].astype(F32)) * ya + _sigmoid(gb_ref[...].astype(F32)) * yb
    mix = jnp.dot(y.astype(BF16), wo_ref[...], preferred_element_type=F32)
    m = mod_ref[0]
    x1 = x_ref[...] + m[2:3] * mix
    x1_ref[...] = x1
    ms = jnp.mean(x1 * x1, axis=-1, keepdims=True)
    h2 = x1 * lax.rsqrt(ms + EPS) * nw_ref[...]
    h2 = h2 * (1.0 + m[4:5]) + m[3:4]
    h2_ref[...] = h2.astype(BF16)
    gates_ref[...] = _route(h2, rwt_ref, rb_ref)


def _mix(hm, hs, p, x2d, mod_l, wa, wb, wo, norm2_w, rwt, rb, seq):
    n = x2d.shape[0]
    tiles_per_seq = seq // TM_MIX
    tile = lambda i: (i, 0)
    const = lambda i: (0, 0)
    return pl.pallas_call(
        _mix_kernel,
        grid=(n // TM_MIX,),
        in_specs=[pl.BlockSpec((TM_MIX, D), tile),
                  pl.BlockSpec((TM_MIX, D), tile),
                  pl.BlockSpec((TM_MIX, D), lambda i: (i, P_GA)),
                  pl.BlockSpec((TM_MIX, D), lambda i: (i, P_GB)),
                  pl.BlockSpec((TM_MIX, D), tile),
                  pl.BlockSpec((1, 6, D), lambda i: (i // tiles_per_seq, 0, 0)),
                  pl.BlockSpec((D, D), const),
                  pl.BlockSpec((D, D), const),
                  pl.BlockSpec((D, D), const),
                  pl.BlockSpec((1, D), const),
                  pl.BlockSpec((N_EXPERTS, D), const),
                  pl.BlockSpec((N_EXPERTS, 1), const)],
        out_specs=[pl.BlockSpec((TM_MIX, D), tile),
                   pl.BlockSpec((TM_MIX, D), tile),
                   pl.BlockSpec((TM_MIX, LANES), tile)],
        out_shape=[jax.ShapeDtypeStruct((n, D), F32),
                   jax.ShapeDtypeStruct((n, D), BF16),
                   jax.ShapeDtypeStruct((n, LANES), F32)],
        compiler_params=_params(("parallel",)),
        name="mix_out",
    )(hm, hs, p, p, x2d, mod_l, wa, wb, wo, norm2_w.reshape(1, D), rwt, rb)


def _moe_kernel(h2_ref, gates_ref, x1_ref, mod_ref, wg_ref, wu_ref, wd_ref,
                sg_ref, su_ref, sd_ref, fnw_ref, o_ref, wg_b, wu_b, wd_b, *, final_norm):
    e = pl.program_id(1)
    chunks = [pl.ds(r * R_MOE, R_MOE) for r in range(TM_MOE // R_MOE)]

    def swiglu(rows, gate_cols):
        xr = h2_ref[rows, :]
        g = jnp.dot(xr, wg_b[...], preferred_element_type=F32)
        u = jnp.dot(xr, wu_b[...], preferred_element_type=F32)
        hid = g * _sigmoid(g) * u
        if gate_cols is not None:
            hid = hid * gate_cols
        return jnp.dot(hid.astype(BF16), wd_b[...], preferred_element_type=F32)

    @pl.when(e == 0)
    def _():
        wg_b[...] = sg_ref[0].astype(BF16)
        wu_b[...] = su_ref[0].astype(BF16)
        wd_b[...] = sd_ref[0].astype(BF16)
        for rows in chunks:
            o_ref[rows, :] = swiglu(rows, None)

    wg_b[:, :EXPERT_FF] = wg_ref[0, 0].astype(BF16)
    wg_b[:, EXPERT_FF:] = wg_ref[0, 1].astype(BF16)
    wu_b[:, :EXPERT_FF] = wu_ref[0, 0].astype(BF16)
    wu_b[:, EXPERT_FF:] = wu_ref[0, 1].astype(BF16)
    wd_b[:EXPERT_FF, :] = wd_ref[0, 0].astype(BF16)
    wd_b[EXPERT_FF:, :] = wd_ref[0, 1].astype(BF16)
    lane = lax.broadcasted_iota(jnp.int32, (1, LANES), 1)
    first_half = lax.broadcasted_iota(jnp.int32, (1, 2 * EXPERT_FF), 1) < EXPERT_FF
    for rows in chunks:
        gts = gates_ref[rows, :]
        g0 = jnp.sum(jnp.where(lane == 2 * e, gts, 0.0), axis=1, keepdims=True)
        g1 = jnp.sum(jnp.where(lane == 2 * e + 1, gts, 0.0), axis=1, keepdims=True)
        o_ref[rows, :] += swiglu(rows, jnp.where(first_half, g0, g1))

    @pl.when(e == pl.num_programs(1) - 1)
    def _():
        x2 = x1_ref[...] + mod_ref[0][5:6] * o_ref[...]
        if final_norm:
            ms = jnp.mean(x2 * x2, axis=-1, keepdims=True)
            x2 = x2 * lax.rsqrt(ms + EPS) * fnw_ref[...]
        o_ref[...] = x2


def _moe(h2, gates, x1, mod_l, layer, w_gate, w_up, w_down, sh_gate, sh_up, sh_down, fnw, seq, final_norm):
    n = h2.shape[0]
    tiles_per_seq = seq // TM_MOE
    tile = lambda i, e: (i, 0)
    once = pl.Buffered(1)
    return pl.pallas_call(
        functools.partial(_moe_kernel, final_norm=final_norm),
        grid=(n // TM_MOE, N_EXPERTS // 2),
        in_specs=[pl.BlockSpec((TM_MOE, D), tile, pipeline_mode=once),
                  pl.BlockSpec((TM_MOE, LANES), tile, pipeline_mode=once),
                  pl.BlockSpec((TM_MOE, D), tile, pipeline_mode=once),
                  pl.BlockSpec((1, 6, D), lambda i, e: (i // tiles_per_seq, 0, 0)),
                  pl.BlockSpec((1, 2, D, EXPERT_FF), lambda i, e: (layer, e, 0, 0)),
                  pl.BlockSpec((1, 2, D, EXPERT_FF), lambda i, e: (layer, e, 0, 0)),
                  pl.BlockSpec((1, 2, EXPERT_FF, D), lambda i, e: (layer, e, 0, 0)),
                  pl.BlockSpec((1, D, SHARED_FF), lambda i, e: (layer, 0, 0), pipeline_mode=once),
                  pl.BlockSpec((1, D, SHARED_FF), lambda i, e: (layer, 0, 0), pipeline_mode=once),
                  pl.BlockSpec((1, SHARED_FF, D), lambda i, e: (layer, 0, 0), pipeline_mode=once),
                  pl.BlockSpec((1, D), lambda i, e: (0, 0))],
        out_specs=pl.BlockSpec((TM_MOE, D), tile),
        out_shape=jax.ShapeDtypeStruct((n, D), F32),
        scratch_shapes=[pltpu.VMEM((D, 2 * EXPERT_FF), BF16),
                        pltpu.VMEM((D, 2 * EXPERT_FF), BF16),
                        pltpu.VMEM((2 * EXPERT_FF, D), BF16)],
        compiler_params=_params(("parallel", "arbitrary")),
        name="moe",
    )(h2, gates, x1, mod_l, w_gate, w_up, w_down, sh_gate, sh_up, sh_down, fnw.reshape(1, D))


def kernel(x, c, ada_w, ada_b, norm1_w, w_in, conv_w, conv_b, m_igate_b, m_fgate_b, m_norm_w, w_proj_a, w_proj_b, w_out, norm2_w, router_w, router_b, w_gate, w_up, w_down, sh_gate, sh_up, sh_down, final_norm_w):
    bsz, seq, _ = x.shape
    n = bsz * seq
    depth = ada_w.shape[0]
    assert 2 * N_MHEADS == SUBLANES and SHARED_FF == 2 * EXPERT_FF and seq % TM_IN == 0 and seq % L_CHUNK == 0 and seq % TQ == 0 and seq >= WIN and WIN % KB == 0 and TQ % KB == 0

    c_pad = jnp.zeros((SUBLANES, D), F32).at[:bsz].set(c)
    mod = _ada(c_pad, ada_w, ada_b)[:, :bsz].reshape(depth, bsz, 6, D)

    gate_lo = 2 * N_MHEADS * DK + 2 * N_MHEADS * DV
    gate_hi = gate_lo + 2 * N_MHEADS
    xc = x.reshape(n, D)
    for l in range(depth):
        w_main, wg_col, wg_row = _repack_w_in(w_in, l, gate_lo, gate_hi - gate_lo)
        gate_b = jnp.concatenate([m_igate_b[l], m_fgate_b[l]]).astype(F32)
        brow = jnp.zeros((1, LANES), F32).at[0, :2 * N_MHEADS].set(gate_b)

        p, gcol, grow = _inproj(xc, mod[l], norm1_w[l], w_main, wg_col, wg_row, seq)
        hm = _mlstm(p, gcol, grow, conv_w[l], conv_b[l], brow, gate_b.reshape(2 * N_MHEADS, 1),
                    m_norm_w[l], bsz, seq)
        hs = _sb_attention(p, bsz, seq)
        x1, h2, gates = _mix(hm, hs, p, xc, mod[l], w_proj_a[l].astype(BF16), w_proj_b[l].astype(BF16),
                             w_out[l].astype(BF16), norm2_w[l], router_w[l].T,
                             router_b[l].reshape(N_EXPERTS, 1), seq)
        xc = _moe(h2, gates, x1, mod[l], l, w_gate, w_up, w_down, sh_gate, sh_up, sh_down,
                  final_norm_w, seq, final_norm=(l == depth - 1))
    return xc.reshape(bsz, seq, D)
```

```python
import functools

import jax
import jax.numpy as jnp
from jax import lax
from jax.experimental import pallas as pl
from jax.experimental.pallas import tpu as pltpu

F32 = jnp.float32
BF16 = jnp.bfloat16

D = 1024
DEPTH = 2
N_MHEADS = 4
DK = 128
DV = 256
CONV_K = 4
SB_HEADS = 16
SB_DH = 64
N_EXPERTS = 64
TOP_K = 8
N_GROUPS = 8
TOPK_GROUPS = 4
GROUP_SIZE = N_EXPERTS // N_GROUPS
EXPERT_FF = 128
SHARED_FF = 256
ROUTED_SCALE = 2.5
EPS = 1e-6

LANES = 128
SUBLANES = 8
VMEM_LIMIT = 56 * 1024 * 1024

P_MQK, P_MV, P_MO, P_SQ, P_SK, P_SV, P_GA, P_GB = range(8)

TM_IN = 2048
TN_IN = 1024
L_CHUNK = 256
TQ = 1024
AV_LAG = 4
KB = 128
WIN = 384
SKIP_LOG = -104.0
TM_MIX = 1024
R_MIX = 512
TM_MOE = 2048
R_MOE = 1024
NEG_INF = float("-inf")
LOG2E = 1.4426950408889634


def _sigmoid(x):
    return 0.5 * jnp.tanh(0.5 * x) + 0.5


def _softplus(x):
    return jnp.maximum(x, 0.0) + jnp.log(1.0 + jnp.exp2(jnp.abs(x) * (-LOG2E)))


def _params(sem):
    return pltpu.CompilerParams(dimension_semantics=sem, vmem_limit_bytes=VMEM_LIMIT)


def _ada_kernel(c_ref, w_ref, b_ref, o_ref):
    c = c_ref[...]
    cond = c * _sigmoid(c)
    o_ref[0] = jnp.dot(cond, w_ref[0], preferred_element_type=F32,
                       precision=lax.Precision.HIGHEST) + b_ref[0]


def _ada(c_pad, ada_w, ada_b):
    tn = 1536
    depth = ada_w.shape[0]
    return pl.pallas_call(
        _ada_kernel,
        grid=(depth, 6 * D // tn),
        in_specs=[pl.BlockSpec((SUBLANES, D), lambda l, j: (0, 0)),
                  pl.BlockSpec((1, D, tn), lambda l, j: (l, 0, j)),
                  pl.BlockSpec((1, 1, tn), lambda l, j: (l, 0, j))],
        out_specs=pl.BlockSpec((1, SUBLANES, tn), lambda l, j: (l, 0, j)),
        out_shape=jax.ShapeDtypeStruct((depth, SUBLANES, 6 * D), F32),
        compiler_params=_params(("parallel", "parallel")),
        name="ada_mod",
    )(c_pad, ada_w, ada_b.reshape(depth, 1, 6 * D))


def _repack_kernel(w_ref, main_ref, gcol_ref, grow_ref, *, gate_lo, n_gate):
    w = w_ref[0]
    width = w.shape[1]
    main_ref[:, :gate_lo] = w[:, :gate_lo].astype(BF16)
    main_ref[:, gate_lo:] = w[:, gate_lo + n_gate:width].astype(BF16)
    lane = lax.broadcasted_iota(jnp.int32, (1, LANES), 1)
    g = jnp.where(lane < n_gate, w[:, gate_lo:gate_lo + LANES], 0.0)
    gcol_ref[...] = g.astype(BF16)
    grow_ref[...] = g.T[:SUBLANES, :].astype(BF16)


def _repack_w_in(w_in, layer, gate_lo, n_gate):
    depth, d, width = w_in.shape
    tr = 256
    return pl.pallas_call(
        functools.partial(_repack_kernel, gate_lo=gate_lo, n_gate=n_gate),
        grid=(d // tr,),
        in_specs=[pl.BlockSpec((1, tr, width), lambda i: (layer, i, 0))],
        out_specs=[pl.BlockSpec((tr, width - n_gate), lambda i: (i, 0)),
                   pl.BlockSpec((tr, LANES), lambda i: (i, 0)),
                   pl.BlockSpec((SUBLANES, tr), lambda i: (0, i))],
        out_shape=[jax.ShapeDtypeStruct((d, width - n_gate), BF16),
                   jax.ShapeDtypeStruct((d, LANES), BF16),
                   jax.ShapeDtypeStruct((SUBLANES, d), BF16)],
        compiler_params=_params(("parallel",)),
        name="repack_w_in",
    )(w_in)


def _inproj_kernel(x_ref, mod_ref, nw_ref, w_ref, wgc_ref, wgr_ref,
                   p_ref, gcol_ref, grow_ref, h_scr):
    @pl.when(pl.program_id(1) == 0)
    def _():
        x = x_ref[...]
        ms = jnp.mean(x * x, axis=-1, keepdims=True)
        y = x * lax.rsqrt(ms + EPS) * nw_ref[...]
        m = mod_ref[0]
        hb = (y * (1.0 + m[1:2]) + m[0:1]).astype(BF16)
        h_scr[...] = hb
        gcol_ref[...] = jnp.dot(hb, wgc_ref[...], preferred_element_type=F32)
        grow_ref[...] = lax.dot_general(wgr_ref[...], hb, (((1,), (1,)), ((), ())),
                                        preferred_element_type=F32)

    p_ref[...] = jnp.dot(h_scr[...], w_ref[...], preferred_element_type=F32).astype(BF16)


def _inproj(x2d, mod_l, norm_w, w_main, wg_col, wg_row, seq):
    n = x2d.shape[0]
    tiles_per_seq = seq // TM_IN
    return pl.pallas_call(
        _inproj_kernel,
        grid=(n // TM_IN, w_main.shape[1] // TN_IN),
        in_specs=[pl.BlockSpec((TM_IN, D), lambda i, j: (i, 0)),
                  pl.BlockSpec((1, 6, D), lambda i, j: (i // tiles_per_seq, 0, 0)),
                  pl.BlockSpec((1, D), lambda i, j: (0, 0)),
                  pl.BlockSpec((D, TN_IN), lambda i, j: (0, j)),
                  pl.BlockSpec((D, LANES), lambda i, j: (0, 0)),
                  pl.BlockSpec((SUBLANES, D), lambda i, j: (0, 0))],
        out_specs=[pl.BlockSpec((TM_IN, TN_IN), lambda i, j: (i, j)),
                   pl.BlockSpec((TM_IN, LANES), lambda i, j: (i, 0)),
                   pl.BlockSpec((SUBLANES, TM_IN), lambda i, j: (0, i))],
        out_shape=[jax.ShapeDtypeStruct((n, w_main.shape[1]), BF16),
                   jax.ShapeDtypeStruct((n, LANES), F32),
                   jax.ShapeDtypeStruct((SUBLANES, n), F32)],
        scratch_shapes=[pltpu.VMEM((TM_IN, D), BF16)],
        compiler_params=_params(("parallel", "arbitrary")),
        name="in_proj",
    )(x2d, mod_l, norm_w.reshape(1, D), w_main, wg_col, wg_row)


def _mlstm_kernel(mqk_ref, mv_ref, mo_ref, gcol_ref, grow_ref, cw_ref, cb_ref, shift_ref,
                  brow_ref, bcol_ref, nw_ref, o_ref, tail_scr, c_scr, m_scr):
    L = L_CHUNK

    @pl.when(pl.program_id(1) == 0)
    def _():
        tail_scr[...] = jnp.zeros_like(tail_scr)
        c_scr[...] = jnp.zeros_like(c_scr)
        m_scr[...] = jnp.zeros_like(m_scr)

    cur_b = mqk_ref[...]
    cur = cur_b.astype(F32)
    tail = tail_scr[...]
    row8 = lax.broadcasted_iota(jnp.int32, (SUBLANES, 1), 0)
    conv = cb_ref[...] + cur * cw_ref[CONV_K - 1:CONV_K, :]
    head = jnp.zeros((SUBLANES, D), F32)
    for d in range(1, CONV_K):
        w_d = cw_ref[CONV_K - 1 - d:CONV_K - d, :]
        conv = conv + jnp.dot(shift_ref[d - 1], cur_b, preferred_element_type=F32) * w_d
        head = head + jnp.where(row8 < d, pltpu.roll(tail, d, axis=0), 0.0) * w_d
    conv = jnp.concatenate([conv[:SUBLANES] + head, conv[SUBLANES:]], axis=0)
    tail_scr[...] = cur[L - SUBLANES:, :]
    qk = conv * _sigmoid(conv)

    gcol = gcol_ref[...] + brow_ref[...]
    grow = grow_ref[...] + bcol_ref[...]
    lf_col_all = -_softplus(-gcol)
    lf_row_all = -_softplus(-grow)

    r_idx = lax.broadcasted_iota(jnp.int32, (L, L), 0)
    c_idx = lax.broadcasted_iota(jnp.int32, (L, L), 1)
    causal = c_idx <= r_idx
    one_col = jnp.where(lax.broadcasted_iota(jnp.int32, (L, LANES), 1) == 0, 1.0, 0.0).astype(BF16)

    for h in range(N_MHEADS):
        ig_row = grow[h:h + 1, :]
        lf_row = lf_row_all[N_MHEADS + h:N_MHEADS + h + 1, :]
        ig_col = gcol[:, h:h + 1]
        lf_col = lf_col_all[:, N_MHEADS + h:N_MHEADS + h + 1]
        m_prev = m_scr[h:h + 1, 0:1]

        b_row = jnp.sum(jnp.where(c_idx >= r_idx, lf_col, 0.0), axis=0, keepdims=True)
        b_col = jnp.sum(jnp.where(causal, lf_row, 0.0), axis=1, keepdims=True)
        u_row = ig_row - b_row
        u_col = ig_col - b_col
        cm_col = jnp.maximum(m_prev, jnp.max(jnp.where(causal, u_row, NEG_INF), axis=1, keepdims=True))
        cm_last = jnp.maximum(m_prev, jnp.max(u_row, axis=1, keepdims=True))
        a_tot = jnp.sum(lf_row, axis=1, keepdims=True)

        w_intra = jnp.where(causal, jnp.exp(u_row - cm_col), 0.0)
        w_inter = jnp.exp(m_prev - cm_col)
        clamp = jnp.exp(-(b_col + cm_col))
        wc_col = jnp.exp(u_col - cm_last)
        decay = jnp.exp(m_prev - cm_last)

        q_h = (qk[:, h * DK:(h + 1) * DK] * (DK ** -0.5)).astype(BF16)
        k_f = qk[:, N_MHEADS * DK + h * DK:N_MHEADS * DK + (h + 1) * DK]
        k_h = k_f.astype(BF16)
        v_aug = jnp.concatenate([mv_ref[:, h * DV:(h + 1) * DV], one_col], axis=1)

        s = lax.dot_general(q_h, k_h, (((1,), (1,)), ((), ())), preferred_element_type=F32)
        s = (s * w_intra).astype(BF16)
        cn = c_scr[h]
        nd = jnp.dot(s, v_aug, preferred_element_type=F32)
        nd = nd + w_inter * jnp.dot(q_h, cn.astype(BF16), preferred_element_type=F32)
        num = nd[:, :DV]
        den = nd[:, DV:DV + 1]
        hh = num / jnp.maximum(jnp.abs(den), clamp)

        kw_t = (k_f * wc_col).T.astype(BF16)
        c_scr[h] = decay * cn + jnp.dot(kw_t, v_aug, preferred_element_type=F32)
        m_scr[h:h + 1, :] = jnp.broadcast_to(a_tot + cm_last, (1, LANES))

        hn = hh * lax.rsqrt(jnp.mean(hh * hh, axis=-1, keepdims=True) + EPS)
        hn = hn * nw_ref[:, h * DV:(h + 1) * DV]
        og = _sigmoid(mo_ref[:, h * DV:(h + 1) * DV].astype(F32))
        o_ref[:, h * DV:(h + 1) * DV] = (hn * og).astype(BF16)


def _mlstm(p, gcol, grow, conv_w, conv_b, brow, bcol, m_norm_w, bsz, seq):
    n = p.shape[0]
    nc = seq // L_CHUNK
    row = lambda b, c: b * nc + c
    t_idx = lax.broadcasted_iota(jnp.int32, (CONV_K - 1, L_CHUNK, L_CHUNK), 1)
    s_idx = lax.broadcasted_iota(jnp.int32, (CONV_K - 1, L_CHUNK, L_CHUNK), 2)
    d_idx = lax.broadcasted_iota(jnp.int32, (CONV_K - 1, L_CHUNK, L_CHUNK), 0) + 1
    shifts = jnp.where(t_idx - s_idx == d_idx, 1.0, 0.0).astype(BF16)
    return pl.pallas_call(
        _mlstm_kernel,
        grid=(bsz, nc),
        in_specs=[pl.BlockSpec((L_CHUNK, D), lambda b, c: (row(b, c), P_MQK)),
                  pl.BlockSpec((L_CHUNK, D), lambda b, c: (row(b, c), P_MV)),
                  pl.BlockSpec((L_CHUNK, D), lambda b, c: (row(b, c), P_MO)),
                  pl.BlockSpec((L_CHUNK, LANES), lambda b, c: (row(b, c), 0)),
                  pl.BlockSpec((SUBLANES, L_CHUNK), lambda b, c: (0, row(b, c))),
                  pl.BlockSpec((CONV_K, D), lambda b, c: (0, 0)),
                  pl.BlockSpec((1, D), lambda b, c: (0, 0)),
                  pl.BlockSpec((CONV_K - 1, L_CHUNK, L_CHUNK), lambda b, c: (0, 0, 0)),
                  pl.BlockSpec((1, LANES), lambda b, c: (0, 0)),
                  pl.BlockSpec((SUBLANES, 1), lambda b, c: (0, 0)),
                  pl.BlockSpec((1, D), lambda b, c: (0, 0))],
        out_specs=pl.BlockSpec((L_CHUNK, D), lambda b, c: (row(b, c), 0)),
        out_shape=jax.ShapeDtypeStruct((n, D), BF16),
        scratch_shapes=[pltpu.VMEM((SUBLANES, D), F32),
                        pltpu.VMEM((N_MHEADS, DK, DV + LANES), F32),
                        pltpu.VMEM((SUBLANES, LANES), F32)],
        compiler_params=_params(("parallel", "arbitrary")),
        name="mlstm",
    )(p, p, p, gcol, grow, conv_w, conv_b.reshape(1, D), shifts, brow, bcol, m_norm_w.reshape(1, D))


def _sb_cum(sp_blocks, tri):
    rows = []
    for sp in sp_blocks:
        rows.append(sp.astype(BF16))
    lhs = rows[0] if len(rows) == 1 else jnp.concatenate(rows, axis=0)
    cum = jnp.dot(lhs, tri, preferred_element_type=F32)
    return [cum[r * KB:(r + 1) * KB] for r in range(len(sp_blocks))]


def _nt_dot(a, b):
    return lax.dot_general(a, b, (((1,), (1,)), ((), ())), preferred_element_type=F32)


def _sb_tail(qm2, accs, carries, n_rem, k_ref, v_ref, tri):
    def live(c0, c1):
        return (jnp.max(jnp.maximum(c0, c1)) > SKIP_LOG).astype(jnp.int32)

    def cond(st):
        return jnp.logical_and(st[0] < n_rem, st[1] > 0)

    def body(st):
        j, _, a0, a1, c0, c1 = st
        k0 = pl.multiple_of((n_rem - 1 - j) * KB, KB)
        k_blk = k_ref[pl.ds(k0, KB), :]
        v_blk = v_ref[pl.ds(k0, KB), :]
        z_b = [_nt_dot(qm, k_blk) for qm in qm2]
        cum_b = _sb_cum([_softplus(z) for z in z_b], tri)
        acc_b, car_b = [a0, a1], [c0, c1]
        for hd in range(2):
            att = jnp.exp(z_b[hd] + cum_b[hd][:, :KB] + car_b[hd]).astype(BF16)
            acc_b[hd] = acc_b[hd] + jnp.dot(att, v_blk, preferred_element_type=F32)
            car_b[hd] = car_b[hd] + cum_b[hd][:, KB:]
        return (j + 1, live(car_b[0], car_b[1]), acc_b[0], acc_b[1], car_b[0], car_b[1])

    st = lax.while_loop(cond, body, (jnp.int32(0), live(carries[0], carries[1]),
                                     accs[0], accs[1], carries[0], carries[1]))
    return st[2], st[3]


def _sb_kernel(q_ref, k_ref, v_ref, tri_ref, o_ref):
    i = pl.program_id(2)
    tri = tri_ref[...]
    lane = lax.broadcasted_iota(jnp.int32, (1, LANES), 1)
    q2 = q_ref[...] * jnp.asarray(SB_DH ** -0.5, BF16)
    qms = [jnp.where((lane >= hd * SB_DH) & (lane < (hd + 1) * SB_DH), q2, jnp.zeros_like(q2))
           for hd in range(2)]
    nsub = TQ // KB
    nb = WIN // KB
    diff = (lax.broadcasted_iota(jnp.int32, (KB, WIN), 1)
            - lax.broadcasted_iota(jnp.int32, (KB, WIN), 0))
    below_diag = diff[:, :KB] < 0

    st = [dict() for _ in range(nsub)]

    def stage_scores(s):
        qs = i * TQ + s * KB
        ws = pl.multiple_of(jnp.maximum(qs - (WIN - KB), 0), KB)
        if s * KB >= WIN - KB:
            masks = [None] * (nb - 1) + [below_diag]
        else:
            valid = diff < (qs - ws)
            masks = [valid[:, kb * KB:(kb + 1) * KB] for kb in range(nb)]
        k_win = k_ref[pl.ds(ws, WIN), :]
        zs = [_nt_dot(qms[hd][s * KB:(s + 1) * KB], k_win) for hd in range(2)]
        sps = []
        for z in zs:
            for kb in range(nb):
                sp = _softplus(z[:, kb * KB:(kb + 1) * KB])
                sps.append(sp if masks[kb] is None else jnp.where(masks[kb], sp, 0.0))
        st[s].update(masks=masks, z=zs, cum=_sb_cum(sps, tri), v=v_ref[pl.ds(ws, WIN), :], n_rem=ws // KB)

    def stage_output(s):
        st[s]["acc"], st[s]["carry"] = [], []
        for hd in range(2):
            carry = jnp.zeros((KB, KB), F32)
            att = [None] * nb
            for kb in reversed(range(nb)):
                cum = st[s]["cum"][hd * nb + kb]
                w = jnp.exp(st[s]["z"][hd][:, kb * KB:(kb + 1) * KB] + cum[:, :KB] + carry)
                m = st[s]["masks"][kb]
                att[kb] = (w if m is None else jnp.where(m, w, 0.0)).astype(BF16)
                carry = carry + cum[:, KB:]
            st[s]["acc"].append(jnp.dot(jnp.concatenate(att, axis=1), st[s]["v"], preferred_element_type=F32))
            st[s]["carry"].append(carry)

    for t in range(nsub + AV_LAG):
        if t < nsub:
            stage_scores(t)
        if 0 <= t - AV_LAG < nsub:
            stage_output(t - AV_LAG)
    accs = [a for s in range(nsub) for a in st[s]["acc"]]
    carries = [c for s in range(nsub) for c in st[s]["carry"]]
    n_rems = [st[s]["n_rem"] for s in range(nsub)]

    worst = functools.reduce(jnp.maximum, carries)
    need_tail = jnp.logical_and(jnp.max(worst) > SKIP_LOG, n_rems[-1] > 0)

    def run_tails():
        res = []
        for s in range(nsub):
            qm2 = [qms[hd][s * KB:(s + 1) * KB] for hd in range(2)]
            res.extend(_sb_tail(qm2, accs[2 * s:2 * s + 2], carries[2 * s:2 * s + 2],
                                n_rems[s], k_ref, v_ref, tri))
        return tuple(res)

    accs = lax.cond(need_tail, run_tails, lambda: tuple(accs))
    o_ref[...] = jnp.concatenate(
        [jnp.where(lane < SB_DH, accs[2 * s], accs[2 * s + 1]).astype(BF16) for s in range(nsub)], axis=0)


def _sb_attention(p, bsz, seq):
    n = p.shape[0]
    nq = seq // TQ
    pairs = SB_HEADS * SB_DH // LANES
    per_blk = D // LANES
    rr = lax.broadcasted_iota(jnp.int32, (KB, 2 * KB), 0)
    cc = lax.broadcasted_iota(jnp.int32, (KB, 2 * KB), 1)
    tri = jnp.where((cc >= KB) | (rr >= cc), -1.0, 0.0).astype(BF16)
    return pl.pallas_call(
        _sb_kernel,
        grid=(bsz, pairs, nq),
        in_specs=[pl.BlockSpec((TQ, LANES), lambda b, g, i: (b * nq + i, P_SQ * per_blk + g)),
                  pl.BlockSpec((seq, LANES), lambda b, g, i: (b, P_SK * per_blk + g)),
                  pl.BlockSpec((seq, LANES), lambda b, g, i: (b, P_SV * per_blk + g)),
                  pl.BlockSpec((KB, 2 * KB), lambda b, g, i: (0, 0))],
        out_specs=pl.BlockSpec((TQ, LANES), lambda b, g, i: (b * nq + i, g)),
        out_shape=jax.ShapeDtypeStruct((n, D), BF16),
        compiler_params=_params(("parallel", "parallel", "arbitrary")),
        name="sb_attention",
    )(p, p, p, tri)


def _route_logits(h2, rwt_ref):
    return lax.dot_general(rwt_ref[...], h2, (((1,), (1,)), ((), ())),
                           preferred_element_type=F32, precision=lax.Precision.HIGHEST)


def _route_select(logits, rb_ref):
    tm = logits.shape[1]
    scores = _sigmoid(logits)
    sel = scores + rb_ref[...]
    sub = lax.broadcasted_iota(jnp.int32, (GROUP_SIZE, tm), 0).astype(F32)
    grp_scores = []
    for g in range(N_GROUPS):
        x = sel[g * GROUP_SIZE:(g + 1) * GROUP_SIZE, :]
        m1 = jnp.max(x, axis=0, keepdims=True)
        first = jnp.min(jnp.where(x == m1, sub, float(GROUP_SIZE)), axis=0, keepdims=True)
        m2 = jnp.max(jnp.where(sub == first, NEG_INF, x), axis=0, keepdims=True)
        grp_scores.append(m1 + m2)
    gs = jnp.concatenate(grp_scores, axis=0)
    g_iota = lax.broadcasted_iota(jnp.int32, (N_GROUPS, tm), 0)
    g_rank = jnp.zeros((N_GROUPS, tm), F32)
    for g in range(N_GROUPS):
        row = gs[g:g + 1, :]
        g_rank = g_rank + jnp.where(row > gs, 1.0, jnp.where(row == gs, jnp.where(g_iota > g, 1.0, 0.0), 0.0))
    g_sel = g_rank < TOPK_GROUPS
    masked = jnp.concatenate(
        [jnp.where(g_sel[g:g + 1, :], sel[g * GROUP_SIZE:(g + 1) * GROUP_SIZE, :], NEG_INF)
         for g in range(N_GROUPS)], axis=0)
    e_iota = lax.broadcasted_iota(jnp.int32, (N_EXPERTS, tm), 0).astype(F32)
    chosen = jnp.zeros((N_EXPERTS, tm), F32)
    rest = masked
    for _ in range(TOP_K):
        top = jnp.max(rest, axis=0, keepdims=True)
        first = jnp.min(jnp.where(rest == top, e_iota, float(N_EXPERTS)), axis=0, keepdims=True)
        hit = e_iota == first
        chosen = jnp.where(hit, 1.0, chosen)
        rest = jnp.where(hit, NEG_INF, rest)
    w_sel = jnp.where(chosen > 0.0, scores, 0.0)
    gates_t = w_sel / jnp.sum(w_sel, axis=0, keepdims=True) * ROUTED_SCALE
    gates_t = jnp.concatenate([gates_t, jnp.zeros((LANES - N_EXPERTS, tm), F32)], axis=0)
    return gates_t.T


def _mix_kernel(hm_ref, hs_ref, ga_ref, gb_ref, x_ref, mod_ref, wa_ref, wb_ref, wo_ref,
                nw_ref, rwt_ref, rb_ref, x1_ref, h2_ref, gates_ref):
    m = mod_ref[0]
    n_chunks = TM_MIX // R_MIX
    st = [dict(rows=pl.ds(c * R_MIX, R_MIX)) for c in range(n_chunks)]

    def branches(c):
        rows = st[c]["rows"]
        st[c]["ya"] = jnp.dot(hm_ref[rows, :], wa_ref[...], preferred_element_type=F32)
        st[c]["yb"] = jnp.dot(hs_ref[rows, :], wb_ref[...], preferred_element_type=F32)

    def merge(c):
        rows = st[c]["rows"]
        y = (_sigmoid(ga_ref[rows, :].astype(F32)) * st[c].pop("ya")
             + _sigmoid(gb_ref[rows, :].astype(F32)) * st[c].pop("yb"))
        st[c]["y"] = y.astype(BF16)

    def project(c):
        st[c]["mix"] = jnp.dot(st[c].pop("y"), wo_ref[...], preferred_element_type=F32)

    def residual(c):
        rows = st[c]["rows"]
        x1 = x_ref[rows, :] + m[2:3] * st[c].pop("mix")
        x1_ref[rows, :] = x1
        ms = jnp.mean(x1 * x1, axis=-1, keepdims=True)
        h2 = x1 * lax.rsqrt(ms + EPS) * nw_ref[...]
        h2 = h2 * (1.0 + m[4:5]) + m[3:4]
        h2_ref[rows, :] = h2.astype(BF16)
        st[c]["h2"] = h2

    def logits(c):
        st[c]["logits"] = _route_logits(st[c].pop("h2"), rwt_ref)

    def select(c):
        gates_ref[st[c]["rows"], :] = _route_select(st[c].pop("logits"), rb_ref)

    stages = [branches, merge, project, residual, logits, select]
    for t in range(len(stages) + n_chunks - 1):
        for c in range(n_chunks):
            if 0 <= t - c < len(stages):
                stages[t - c](c)


def _mix(hm, hs, p, x2d, mod_l, wa, wb, wo, norm2_w, rwt, rb, seq):
    n = x2d.shape[0]
    tiles_per_seq = seq // TM_MIX
    tile = lambda i: (i, 0)
    const = lambda i: (0, 0)
    return pl.pallas_call(
        _mix_kernel,
        grid=(n // TM_MIX,),
        in_specs=[pl.BlockSpec((TM_MIX, D), tile),
                  pl.BlockSpec((TM_MIX, D), tile),
                  pl.BlockSpec((TM_MIX, D), lambda i: (i, P_GA)),
                  pl.BlockSpec((TM_MIX, D), lambda i: (i, P_GB)),
                  pl.BlockSpec((TM_MIX, D), tile),
                  pl.BlockSpec((1, 6, D), lambda i: (i // tiles_per_seq, 0, 0)),
                  pl.BlockSpec((D, D), const, pipeline_mode=pl.Buffered(1)),
                  pl.BlockSpec((D, D), const, pipeline_mode=pl.Buffered(1)),
                  pl.BlockSpec((D, D), const, pipeline_mode=pl.Buffered(1)),
                  pl.BlockSpec((1, D), const),
                  pl.BlockSpec((N_EXPERTS, D), const),
                  pl.BlockSpec((N_EXPERTS, 1), const)],
        out_specs=[pl.BlockSpec((TM_MIX, D), tile),
                   pl.BlockSpec((TM_MIX, D), tile),
                   pl.BlockSpec((TM_MIX, LANES), tile)],
        out_shape=[jax.ShapeDtypeStruct((n, D), F32),
                   jax.ShapeDtypeStruct((n, D), BF16),
                   jax.ShapeDtypeStruct((n, LANES), F32)],
        compiler_params=_params(("parallel",)),
        name="mix_out",
    )(hm, hs, p, p, x2d, mod_l, wa, wb, wo, norm2_w.reshape(1, D), rwt, rb)


def _moe_kernel(h2_ref, gates_ref, x1_ref, mod_ref, wg_ref, wu_ref, wd_ref,
                sg_ref, su_ref, sd_ref, fnw_ref, o_ref, wg_b, wu_b, wd_b, *, final_norm):
    e = pl.program_id(1)
    chunks = [pl.ds(r * R_MOE, R_MOE) for r in range(TM_MOE // R_MOE)]

    def swiglu(rows, gate_cols):
        xr = h2_ref[rows, :]
        g = jnp.dot(xr, wg_b[...], preferred_element_type=F32)
        u = jnp.dot(xr, wu_b[...], preferred_element_type=F32)
        hid = g * _sigmoid(g) * u
        if gate_cols is not None:
            hid = hid * gate_cols
        return jnp.dot(hid.astype(BF16), wd_b[...], preferred_element_type=F32)

    @pl.when(e == 0)
    def _():
        wg_b[...] = sg_ref[0].astype(BF16)
        wu_b[...] = su_ref[0].astype(BF16)
        wd_b[...] = sd_ref[0].astype(BF16)
        for rows in chunks:
            o_ref[rows, :] = swiglu(rows, None)

    wg_b[:, :EXPERT_FF] = wg_ref[0, 0].astype(BF16)
    wg_b[:, EXPERT_FF:] = wg_ref[0, 1].astype(BF16)
    wu_b[:, :EXPERT_FF] = wu_ref[0, 0].astype(BF16)
    wu_b[:, EXPERT_FF:] = wu_ref[0, 1].astype(BF16)
    wd_b[:EXPERT_FF, :] = wd_ref[0, 0].astype(BF16)
    wd_b[EXPERT_FF:, :] = wd_ref[0, 1].astype(BF16)
    lane = lax.broadcasted_iota(jnp.int32, (1, LANES), 1)
    first_half = lax.broadcasted_iota(jnp.int32, (1, 2 * EXPERT_FF), 1) < EXPERT_FF
    for rows in chunks:
        gts = gates_ref[rows, :]
        g0 = jnp.sum(jnp.where(lane == 2 * e, gts, 0.0), axis=1, keepdims=True)
        g1 = jnp.sum(jnp.where(lane == 2 * e + 1, gts, 0.0), axis=1, keepdims=True)
        o_ref[rows, :] += swiglu(rows, jnp.where(first_half, g0, g1))

    @pl.when(e == pl.num_programs(1) - 1)
    def _():
        x2 = x1_ref[...] + mod_ref[0][5:6] * o_ref[...]
        if final_norm:
            ms = jnp.mean(x2 * x2, axis=-1, keepdims=True)
            x2 = x2 * lax.rsqrt(ms + EPS) * fnw_ref[...]
        o_ref[...] = x2


def _moe(h2, gates, x1, mod_l, layer, w_gate, w_up, w_down, sh_gate, sh_up, sh_down, fnw, seq, final_norm):
    n = h2.shape[0]
    tiles_per_seq = seq // TM_MOE
    tile = lambda i, e: (i, 0)
    once = pl.Buffered(1)
    return pl.pallas_call(
        functools.partial(_moe_kernel, final_norm=final_norm),
        grid=(n // TM_MOE, N_EXPERTS // 2),
        in_specs=[pl.BlockSpec((TM_MOE, D), tile, pipeline_mode=once),
                  pl.BlockSpec((TM_MOE, LANES), tile, pipeline_mode=once),
                  pl.BlockSpec((TM_MOE, D), tile, pipeline_mode=once),
                  pl.BlockSpec((1, 6, D), lambda i, e: (i // tiles_per_seq, 0, 0)),
                  pl.BlockSpec((1, 2, D, EXPERT_FF), lambda i, e: (layer, e, 0, 0)),
                  pl.BlockSpec((1, 2, D, EXPERT_FF), lambda i, e: (layer, e, 0, 0)),
                  pl.BlockSpec((1, 2, EXPERT_FF, D), lambda i, e: (layer, e, 0, 0)),
                  pl.BlockSpec((1, D, SHARED_FF), lambda i, e: (layer, 0, 0), pipeline_mode=once),
                  pl.BlockSpec((1, D, SHARED_FF), lambda i, e: (layer, 0, 0), pipeline_mode=once),
                  pl.BlockSpec((1, SHARED_FF, D), lambda i, e: (layer, 0, 0), pipeline_mode=once),
                  pl.BlockSpec((1, D), lambda i, e: (0, 0))],
        out_specs=pl.BlockSpec((TM_MOE, D), tile),
        out_shape=jax.ShapeDtypeStruct((n, D), F32),
        scratch_shapes=[pltpu.VMEM((D, 2 * EXPERT_FF), BF16),
                        pltpu.VMEM((D, 2 * EXPERT_FF), BF16),
                        pltpu.VMEM((2 * EXPERT_FF, D), BF16)],
        compiler_params=_params(("parallel", "arbitrary")),
        name="moe",
    )(h2, gates, x1, mod_l, w_gate, w_up, w_down, sh_gate, sh_up, sh_down, fnw.reshape(1, D))


def kernel(x, c, ada_w, ada_b, norm1_w, w_in, conv_w, conv_b, m_igate_b, m_fgate_b, m_norm_w, w_proj_a, w_proj_b, w_out, norm2_w, router_w, router_b, w_gate, w_up, w_down, sh_gate, sh_up, sh_down, final_norm_w):
    bsz, seq, _ = x.shape
    n = bsz * seq
    depth = ada_w.shape[0]
    assert 2 * N_MHEADS == SUBLANES and SHARED_FF == 2 * EXPERT_FF and seq % TM_IN == 0 and seq % L_CHUNK == 0 and seq % TQ == 0 and seq >= WIN and WIN % KB == 0 and TQ % KB == 0

    c_pad = jnp.zeros((SUBLANES, D), F32).at[:bsz].set(c)
    mod = _ada(c_pad, ada_w, ada_b)[:, :bsz].reshape(depth, bsz, 6, D)

    gate_lo = 2 * N_MHEADS * DK + 2 * N_MHEADS * DV
    gate_hi = gate_lo + 2 * N_MHEADS
    xc = x.reshape(n, D)
    for l in range(depth):
        w_main, wg_col, wg_row = _repack_w_in(w_in, l, gate_lo, gate_hi - gate_lo)
        gate_b = jnp.concatenate([m_igate_b[l], m_fgate_b[l]]).astype(F32)
        brow = jnp.zeros((1, LANES), F32).at[0, :2 * N_MHEADS].set(gate_b)

        p, gcol, grow = _inproj(xc, mod[l], norm1_w[l], w_main, wg_col, wg_row, seq)
        hm = _mlstm(p, gcol, grow, conv_w[l], conv_b[l], brow, gate_b.reshape(2 * N_MHEADS, 1),
                    m_norm_w[l], bsz, seq)
        hs = _sb_attention(p, bsz, seq)
        x1, h2, gates = _mix(hm, hs, p, xc, mod[l], w_proj_a[l].astype(BF16), w_proj_b[l].astype(BF16),
                             w_out[l].astype(BF16), norm2_w[l], router_w[l].T,
                             router_b[l].reshape(N_EXPERTS, 1), seq)
        xc = _moe(h2, gates, x1, mod[l], l, w_gate, w_up, w_down, sh_gate, sh_up, sh_down,
                  final_norm_w, seq, final_norm=(l == depth - 1))
    return xc.reshape(bsz, seq, D)
```

```python
import functools

import jax
import jax.numpy as jnp
from jax import lax
from jax.experimental import pallas as pl
from jax.experimental.pallas import tpu as pltpu

F32 = jnp.float32
BF16 = jnp.bfloat16

D = 1024
DEPTH = 2
N_MHEADS = 4
DK = 128
DV = 256
CONV_K = 4
SB_HEADS = 16
SB_DH = 64
N_EXPERTS = 64
TOP_K = 8
N_GROUPS = 8
TOPK_GROUPS = 4
GROUP_SIZE = N_EXPERTS // N_GROUPS
EXPERT_FF = 128
SHARED_FF = 256
ROUTED_SCALE = 2.5
EPS = 1e-6

LANES = 128
SUBLANES = 8
VMEM_LIMIT = 56 * 1024 * 1024

P_MQK, P_MV, P_MO, P_SQ, P_SK, P_SV, P_GA, P_GB = range(8)

TM_IN = 2048
TN_IN = 1024
L_CHUNK = 256
TQ = 1024
AV_LAG = 4
KB = 128
WIN = 384
SKIP_LOG = -104.0
TM_MIX = 1024
R_MIX = 512
TM_MOE = 2048
R_MOE = 1024
NEG_INF = float("-inf")
LOG2E = 1.4426950408889634


def _sigmoid(x):
    return 0.5 * jnp.tanh(0.5 * x) + 0.5


def _softplus(x):
    return jnp.maximum(x, 0.0) + jnp.log(1.0 + jnp.exp2(jnp.abs(x) * (-LOG2E)))


def _params(sem):
    return pltpu.CompilerParams(dimension_semantics=sem, vmem_limit_bytes=VMEM_LIMIT)


def _ada_kernel(c_ref, w_ref, b_ref, o_ref):
    c = c_ref[...]
    cond = c * _sigmoid(c)
    o_ref[0] = jnp.dot(cond, w_ref[0], preferred_element_type=F32,
                       precision=lax.Precision.HIGHEST) + b_ref[0]


def _ada(c_pad, ada_w, ada_b):
    tn = 1536
    depth = ada_w.shape[0]
    return pl.pallas_call(
        _ada_kernel,
        grid=(depth, 6 * D // tn),
        in_specs=[pl.BlockSpec((SUBLANES, D), lambda l, j: (0, 0)),
                  pl.BlockSpec((1, D, tn), lambda l, j: (l, 0, j)),
                  pl.BlockSpec((1, 1, tn), lambda l, j: (l, 0, j))],
        out_specs=pl.BlockSpec((1, SUBLANES, tn), lambda l, j: (l, 0, j)),
        out_shape=jax.ShapeDtypeStruct((depth, SUBLANES, 6 * D), F32),
        compiler_params=_params(("parallel", "parallel")),
        name="ada_mod",
    )(c_pad, ada_w, ada_b.reshape(depth, 1, 6 * D))


def _repack_kernel(w_ref, main_ref, gcol_ref, grow_ref, *, gate_lo, n_gate):
    w = w_ref[...]
    width = w.shape[1]
    main_ref[:, :gate_lo] = w[:, :gate_lo].astype(BF16)
    main_ref[:, gate_lo:] = w[:, gate_lo + n_gate:width].astype(BF16)
    lane = lax.broadcasted_iota(jnp.int32, (1, LANES), 1)
    g = jnp.where(lane < n_gate, w[:, gate_lo:gate_lo + LANES], 0.0)
    gcol_ref[...] = g.astype(BF16)
    grow_ref[...] = g.T[:SUBLANES, :].astype(BF16)


def _repack_w_in(w_in, layer, gate_lo, n_gate):
    depth, d, width = w_in.shape
    tr = 256
    return pl.pallas_call(
        functools.partial(_repack_kernel, gate_lo=gate_lo, n_gate=n_gate),
        grid=(d // tr,),
        in_specs=[pl.BlockSpec((tr, width), lambda i: (layer * (d // tr) + i, 0))],
        out_specs=[pl.BlockSpec((tr, width - n_gate), lambda i: (i, 0)),
                   pl.BlockSpec((tr, LANES), lambda i: (i, 0)),
                   pl.BlockSpec((SUBLANES, tr), lambda i: (0, i))],
        out_shape=[jax.ShapeDtypeStruct((d, width - n_gate), BF16),
                   jax.ShapeDtypeStruct((d, LANES), BF16),
                   jax.ShapeDtypeStruct((SUBLANES, d), BF16)],
        compiler_params=_params(("parallel",)),
        name="repack_w_in",
    )(w_in.reshape(depth * d, width))


def _inproj_kernel(x_ref, mod_ref, nw_ref, w_ref, wgc_ref, wgr_ref,
                   p_ref, gcol_ref, grow_ref, h_scr):
    @pl.when(pl.program_id(1) == 0)
    def _():
        x = x_ref[...]
        ms = jnp.mean(x * x, axis=-1, keepdims=True)
        y = x * lax.rsqrt(ms + EPS) * nw_ref[...]
        m = mod_ref[0]
        hb = (y * (1.0 + m[1:2]) + m[0:1]).astype(BF16)
        h_scr[...] = hb
        gcol_ref[...] = jnp.dot(hb, wgc_ref[...], preferred_element_type=F32)
        grow_ref[...] = lax.dot_general(wgr_ref[...], hb, (((1,), (1,)), ((), ())),
                                        preferred_element_type=F32)

    p_ref[...] = jnp.dot(h_scr[...], w_ref[...], preferred_element_type=F32).astype(BF16)


def _inproj(x2d, mod_l, norm_w, w_main, wg_col, wg_row, seq):
    n = x2d.shape[0]
    tiles_per_seq = seq // TM_IN
    return pl.pallas_call(
        _inproj_kernel,
        grid=(n // TM_IN, w_main.shape[1] // TN_IN),
        in_specs=[pl.BlockSpec((TM_IN, D), lambda i, j: (i, 0)),
                  pl.BlockSpec((1, 6, D), lambda i, j: (i // tiles_per_seq, 0, 0)),
                  pl.BlockSpec((1, D), lambda i, j: (0, 0)),
                  pl.BlockSpec((D, TN_IN), lambda i, j: (0, j)),
                  pl.BlockSpec((D, LANES), lambda i, j: (0, 0)),
                  pl.BlockSpec((SUBLANES, D), lambda i, j: (0, 0))],
        out_specs=[pl.BlockSpec((TM_IN, TN_IN), lambda i, j: (i, j)),
                   pl.BlockSpec((TM_IN, LANES), lambda i, j: (i, 0)),
                   pl.BlockSpec((SUBLANES, TM_IN), lambda i, j: (0, i))],
        out_shape=[jax.ShapeDtypeStruct((n, w_main.shape[1]), BF16),
                   jax.ShapeDtypeStruct((n, LANES), F32),
                   jax.ShapeDtypeStruct((SUBLANES, n), F32)],
        scratch_shapes=[pltpu.VMEM((TM_IN, D), BF16)],
        compiler_params=_params(("parallel", "arbitrary")),
        name="in_proj",
    )(x2d, mod_l, norm_w.reshape(1, D), w_main, wg_col, wg_row)


def _mlstm_kernel(mqk_ref, mv_ref, mo_ref, gcol_ref, grow_ref, cw_ref, cb_ref, shift_ref,
                  brow_ref, bcol_ref, nw_ref, o_ref, tail_scr, c_scr, m_scr):
    L = L_CHUNK

    @pl.when(pl.program_id(1) == 0)
    def _():
        tail_scr[...] = jnp.zeros_like(tail_scr)
        c_scr[...] = jnp.zeros_like(c_scr)
        m_scr[...] = jnp.zeros_like(m_scr)

    cur_b = mqk_ref[...]
    cur = cur_b.astype(F32)
    tail = tail_scr[...]
    row8 = lax.broadcasted_iota(jnp.int32, (SUBLANES, 1), 0)
    conv = cb_ref[...] + cur * cw_ref[CONV_K - 1:CONV_K, :]
    head = jnp.zeros((SUBLANES, D), F32)
    for d in range(1, CONV_K):
        w_d = cw_ref[CONV_K - 1 - d:CONV_K - d, :]
        conv = conv + jnp.dot(shift_ref[d - 1], cur_b, preferred_element_type=F32) * w_d
        head = head + jnp.where(row8 < d, pltpu.roll(tail, d, axis=0), 0.0) * w_d
    conv = jnp.concatenate([conv[:SUBLANES] + head, conv[SUBLANES:]], axis=0)
    tail_scr[...] = cur[L - SUBLANES:, :]
    qk = conv * _sigmoid(conv)

    gcol = gcol_ref[...] + brow_ref[...]
    grow = grow_ref[...] + bcol_ref[...]
    lf_col_all = -_softplus(-gcol)
    lf_row_all = -_softplus(-grow)

    r_idx = lax.broadcasted_iota(jnp.int32, (L, L), 0)
    c_idx = lax.broadcasted_iota(jnp.int32, (L, L), 1)
    causal = c_idx <= r_idx
    one_col = jnp.where(lax.broadcasted_iota(jnp.int32, (L, LANES), 1) == 0, 1.0, 0.0).astype(BF16)

    for h in range(N_MHEADS):
        ig_row = grow[h:h + 1, :]
        lf_row = lf_row_all[N_MHEADS + h:N_MHEADS + h + 1, :]
        ig_col = gcol[:, h:h + 1]
        lf_col = lf_col_all[:, N_MHEADS + h:N_MHEADS + h + 1]
        m_prev = m_scr[h:h + 1, 0:1]

        b_row = jnp.sum(jnp.where(c_idx >= r_idx, lf_col, 0.0), axis=0, keepdims=True)
        b_col = jnp.sum(jnp.where(causal, lf_row, 0.0), axis=1, keepdims=True)
        u_row = ig_row - b_row
        u_col = ig_col - b_col
        cm_col = jnp.maximum(m_prev, jnp.max(jnp.where(causal, u_row, NEG_INF), axis=1, keepdims=True))
        cm_last = jnp.maximum(m_prev, jnp.max(u_row, axis=1, keepdims=True))
        a_tot = jnp.sum(lf_row, axis=1, keepdims=True)

        w_intra = jnp.where(causal, jnp.exp(u_row - cm_col), 0.0)
        w_inter = jnp.exp(m_prev - cm_col)
        clamp = jnp.exp(-(b_col + cm_col))
        wc_col = jnp.exp(u_col - cm_last)
        decay = jnp.exp(m_prev - cm_last)

        q_h = (qk[:, h * DK:(h + 1) * DK] * (DK ** -0.5)).astype(BF16)
        k_f = qk[:, N_MHEADS * DK + h * DK:N_MHEADS * DK + (h + 1) * DK]
        k_h = k_f.astype(BF16)
        v_aug = jnp.concatenate([mv_ref[:, h * DV:(h + 1) * DV], one_col], axis=1)

        s = lax.dot_general(q_h, k_h, (((1,), (1,)), ((), ())), preferred_element_type=F32)
        s = (s * w_intra).astype(BF16)
        cn = c_scr[h]
        nd = jnp.dot(s, v_aug, preferred_element_type=F32)
        nd = nd + w_inter * jnp.dot(q_h, cn.astype(BF16), preferred_element_type=F32)
        num = nd[:, :DV]
        den = nd[:, DV:DV + 1]
        hh = num / jnp.maximum(jnp.abs(den), clamp)

        kw_t = (k_f * wc_col).T.astype(BF16)
        c_scr[h] = decay * cn + jnp.dot(kw_t, v_aug, preferred_element_type=F32)
        m_scr[h:h + 1, :] = jnp.broadcast_to(a_tot + cm_last, (1, LANES))

        hn = hh * lax.rsqrt(jnp.mean(hh * hh, axis=-1, keepdims=True) + EPS)
        hn = hn * nw_ref[:, h * DV:(h + 1) * DV]
        og = _sigmoid(mo_ref[:, h * DV:(h + 1) * DV].astype(F32))
        o_ref[:, h * DV:(h + 1) * DV] = (hn * og).astype(BF16)


def _mlstm(p, gcol, grow, conv_w, conv_b, brow, bcol, m_norm_w, bsz, seq):
    n = p.shape[0]
    nc = seq // L_CHUNK
    row = lambda b, c: b * nc + c
    t_idx = lax.broadcasted_iota(jnp.int32, (CONV_K - 1, L_CHUNK, L_CHUNK), 1)
    s_idx = lax.broadcasted_iota(jnp.int32, (CONV_K - 1, L_CHUNK, L_CHUNK), 2)
    d_idx = lax.broadcasted_iota(jnp.int32, (CONV_K - 1, L_CHUNK, L_CHUNK), 0) + 1
    shifts = jnp.where(t_idx - s_idx == d_idx, 1.0, 0.0).astype(BF16)
    return pl.pallas_call(
        _mlstm_kernel,
        grid=(bsz, nc),
        in_specs=[pl.BlockSpec((L_CHUNK, D), lambda b, c: (row(b, c), P_MQK)),
                  pl.BlockSpec((L_CHUNK, D), lambda b, c: (row(b, c), P_MV)),
                  pl.BlockSpec((L_CHUNK, D), lambda b, c: (row(b, c), P_MO)),
                  pl.BlockSpec((L_CHUNK, LANES), lambda b, c: (row(b, c), 0)),
                  pl.BlockSpec((SUBLANES, L_CHUNK), lambda b, c: (0, row(b, c))),
                  pl.BlockSpec((CONV_K, D), lambda b, c: (0, 0)),
                  pl.BlockSpec((1, D), lambda b, c: (0, 0)),
                  pl.BlockSpec((CONV_K - 1, L_CHUNK, L_CHUNK), lambda b, c: (0, 0, 0)),
                  pl.BlockSpec((1, LANES), lambda b, c: (0, 0)),
                  pl.BlockSpec((SUBLANES, 1), lambda b, c: (0, 0)),
                  pl.BlockSpec((1, D), lambda b, c: (0, 0))],
        out_specs=pl.BlockSpec((L_CHUNK, D), lambda b, c: (row(b, c), 0)),
        out_shape=jax.ShapeDtypeStruct((n, D), BF16),
        scratch_shapes=[pltpu.VMEM((SUBLANES, D), F32),
                        pltpu.VMEM((N_MHEADS, DK, DV + LANES), F32),
                        pltpu.VMEM((SUBLANES, LANES), F32)],
        compiler_params=_params(("parallel", "arbitrary")),
        name="mlstm",
    )(p, p, p, gcol, grow, conv_w, conv_b.reshape(1, D), shifts, brow, bcol, m_norm_w.reshape(1, D))


def _sb_cum(sp_blocks, tri):
    rows = []
    for sp in sp_blocks:
        rows.append(sp.astype(BF16))
    lhs = rows[0] if len(rows) == 1 else jnp.concatenate(rows, axis=0)
    cum = jnp.dot(lhs, tri, preferred_element_type=F32)
    return [cum[r * KB:(r + 1) * KB] for r in range(len(sp_blocks))]


def _nt_dot(a, b):
    return lax.dot_general(a, b, (((1,), (1,)), ((), ())), preferred_element_type=F32)


def _sb_tail(qm2, accs, carries, n_rem, k_ref, v_ref, tri):
    def live(c0, c1):
        return (jnp.max(jnp.maximum(c0, c1)) > SKIP_LOG).astype(jnp.int32)

    def cond(st):
        return jnp.logical_and(st[0] < n_rem, st[1] > 0)

    def body(st):
        j, _, a0, a1, c0, c1 = st
        k0 = pl.multiple_of((n_rem - 1 - j) * KB, KB)
        k_blk = k_ref[pl.ds(k0, KB), :]
        v_blk = v_ref[pl.ds(k0, KB), :]
        z_b = [_nt_dot(qm, k_blk) for qm in qm2]
        cum_b = _sb_cum([_softplus(z) for z in z_b], tri)
        acc_b, car_b = [a0, a1], [c0, c1]
        for hd in range(2):
            att = jnp.exp(z_b[hd] + cum_b[hd][:, :KB] + car_b[hd]).astype(BF16)
            acc_b[hd] = acc_b[hd] + jnp.dot(att, v_blk, preferred_element_type=F32)
            car_b[hd] = car_b[hd] + cum_b[hd][:, KB:]
        return (j + 1, live(car_b[0], car_b[1]), acc_b[0], acc_b[1], car_b[0], car_b[1])

    st = lax.while_loop(cond, body, (jnp.int32(0), live(carries[0], carries[1]),
                                     accs[0], accs[1], carries[0], carries[1]))
    return st[2], st[3]


def _sb_kernel(q_ref, k_ref, v_ref, tri_ref, o_ref):
    i = pl.program_id(2)
    tri = tri_ref[...]
    lane = lax.broadcasted_iota(jnp.int32, (1, LANES), 1)
    q2 = q_ref[...] * jnp.asarray(SB_DH ** -0.5, BF16)
    qms = [jnp.where((lane >= hd * SB_DH) & (lane < (hd + 1) * SB_DH), q2, jnp.zeros_like(q2))
           for hd in range(2)]
    nsub = TQ // KB
    nb = WIN // KB
    diff = (lax.broadcasted_iota(jnp.int32, (KB, WIN), 1)
            - lax.broadcasted_iota(jnp.int32, (KB, WIN), 0))
    below_diag = diff[:, :KB] < 0

    st = [dict() for _ in range(nsub)]

    def stage_scores(s):
        qs = i * TQ + s * KB
        ws = pl.multiple_of(jnp.maximum(qs - (WIN - KB), 0), KB)
        if s * KB >= WIN - KB:
            masks = [None] * (nb - 1) + [below_diag]
        else:
            valid = diff < (qs - ws)
            masks = [valid[:, kb * KB:(kb + 1) * KB] for kb in range(nb)]
        k_win = k_ref[pl.ds(ws, WIN), :]
        zs = [_nt_dot(qms[hd][s * KB:(s + 1) * KB], k_win) for hd in range(2)]
        sps = []
        for z in zs:
            for kb in range(nb):
                sp = _softplus(z[:, kb * KB:(kb + 1) * KB])
                sps.append(sp if masks[kb] is None else jnp.where(masks[kb], sp, 0.0))
        st[s].update(masks=masks, z=zs, cum=_sb_cum(sps, tri), v=v_ref[pl.ds(ws, WIN), :], n_rem=ws // KB)

    def stage_output(s):
        st[s]["acc"], st[s]["carry"] = [], []
        for hd in range(2):
            carry = jnp.zeros((KB, KB), F32)
            att = [None] * nb
            for kb in reversed(range(nb)):
                cum = st[s]["cum"][hd * nb + kb]
                w = jnp.exp(st[s]["z"][hd][:, kb * KB:(kb + 1) * KB] + cum[:, :KB] + carry)
                m = st[s]["masks"][kb]
                att[kb] = (w if m is None else jnp.where(m, w, 0.0)).astype(BF16)
                carry = carry + cum[:, KB:]
            st[s]["acc"].append(jnp.dot(jnp.concatenate(att, axis=1), st[s]["v"], preferred_element_type=F32))
            st[s]["carry"].append(carry)

    for t in range(nsub + AV_LAG):
        if t < nsub:
            stage_scores(t)
        if 0 <= t - AV_LAG < nsub:
            stage_output(t - AV_LAG)
    accs = [a for s in range(nsub) for a in st[s]["acc"]]
    carries = [c for s in range(nsub) for c in st[s]["carry"]]
    n_rems = [st[s]["n_rem"] for s in range(nsub)]

    worst = functools.reduce(jnp.maximum, carries)
    need_tail = jnp.logical_and(jnp.max(worst) > SKIP_LOG, n_rems[-1] > 0)

    def run_tails():
        res = []
        for s in range(nsub):
            qm2 = [qms[hd][s * KB:(s + 1) * KB] for hd in range(2)]
            res.extend(_sb_tail(qm2, accs[2 * s:2 * s + 2], carries[2 * s:2 * s + 2],
                                n_rems[s], k_ref, v_ref, tri))
        return tuple(res)

    accs = lax.cond(need_tail, run_tails, lambda: tuple(accs))
    o_ref[...] = jnp.concatenate(
        [jnp.where(lane < SB_DH, accs[2 * s], accs[2 * s + 1]).astype(BF16) for s in range(nsub)], axis=0)


def _sb_attention(p, bsz, seq):
    n = p.shape[0]
    nq = seq // TQ
    pairs = SB_HEADS * SB_DH // LANES
    per_blk = D // LANES
    rr = lax.broadcasted_iota(jnp.int32, (KB, 2 * KB), 0)
    cc = lax.broadcasted_iota(jnp.int32, (KB, 2 * KB), 1)
    tri = jnp.where((cc >= KB) | (rr >= cc), -1.0, 0.0).astype(BF16)
    return pl.pallas_call(
        _sb_kernel,
        grid=(bsz, pairs, nq),
        in_specs=[pl.BlockSpec((TQ, LANES), lambda b, g, i: (b * nq + i, P_SQ * per_blk + g)),
                  pl.BlockSpec((seq, LANES), lambda b, g, i: (b, P_SK * per_blk + g)),
                  pl.BlockSpec((seq, LANES), lambda b, g, i: (b, P_SV * per_blk + g)),
                  pl.BlockSpec((KB, 2 * KB), lambda b, g, i: (0, 0))],
        out_specs=pl.BlockSpec((TQ, LANES), lambda b, g, i: (b * nq + i, g)),
        out_shape=jax.ShapeDtypeStruct((n, D), BF16),
        compiler_params=_params(("parallel", "parallel", "arbitrary")),
        name="sb_attention",
    )(p, p, p, tri)


def _route_logits(h2, rwt_ref):
    return lax.dot_general(rwt_ref[...], h2, (((1,), (1,)), ((), ())),
                           preferred_element_type=F32, precision=lax.Precision.HIGHEST)


def _route_select(logits, rb_ref):
    tm = logits.shape[1]
    scores = _sigmoid(logits)
    sel = scores + rb_ref[...]
    sub = lax.broadcasted_iota(jnp.int32, (GROUP_SIZE, tm), 0).astype(F32)
    grp_scores = []
    for g in range(N_GROUPS):
        x = sel[g * GROUP_SIZE:(g + 1) * GROUP_SIZE, :]
        m1 = jnp.max(x, axis=0, keepdims=True)
        first = jnp.min(jnp.where(x == m1, sub, float(GROUP_SIZE)), axis=0, keepdims=True)
        m2 = jnp.max(jnp.where(sub == first, NEG_INF, x), axis=0, keepdims=True)
        grp_scores.append(m1 + m2)
    gs = jnp.concatenate(grp_scores, axis=0)
    g_iota = lax.broadcasted_iota(jnp.int32, (N_GROUPS, tm), 0)
    g_rank = jnp.zeros((N_GROUPS, tm), F32)
    for g in range(N_GROUPS):
        row = gs[g:g + 1, :]
        g_rank = g_rank + jnp.where(row > gs, 1.0, jnp.where(row == gs, jnp.where(g_iota > g, 1.0, 0.0), 0.0))
    g_sel = g_rank < TOPK_GROUPS
    masked = jnp.concatenate(
        [jnp.where(g_sel[g:g + 1, :], sel[g * GROUP_SIZE:(g + 1) * GROUP_SIZE, :], NEG_INF)
         for g in range(N_GROUPS)], axis=0)
    e_iota = lax.broadcasted_iota(jnp.int32, (N_EXPERTS, tm), 0).astype(F32)
    chosen = jnp.zeros((N_EXPERTS, tm), F32)
    rest = masked
    for _ in range(TOP_K):
        top = jnp.max(rest, axis=0, keepdims=True)
        first = jnp.min(jnp.where(rest == top, e_iota, float(N_EXPERTS)), axis=0, keepdims=True)
        hit = e_iota == first
        chosen = jnp.where(hit, 1.0, chosen)
        rest = jnp.where(hit, NEG_INF, rest)
    w_sel = jnp.where(chosen > 0.0, scores, 0.0)
    gates_t = w_sel / jnp.sum(w_sel, axis=0, keepdims=True) * ROUTED_SCALE
    gates_t = jnp.concatenate([gates_t, jnp.zeros((LANES - N_EXPERTS, tm), F32)], axis=0)
    return gates_t.T


def _mix_kernel(hm_ref, hs_ref, ga_ref, gb_ref, x_ref, mod_ref, wa_ref, wb_ref, wo_ref,
                nw_ref, rwt_ref, rb_ref, x1_ref, h2_ref, gates_ref):
    m = mod_ref[0]
    n_chunks = TM_MIX // R_MIX
    st = [dict(rows=pl.ds(c * R_MIX, R_MIX)) for c in range(n_chunks)]

    def branches(c):
        rows = st[c]["rows"]
        st[c]["ya"] = jnp.dot(hm_ref[rows, :], wa_ref[...], preferred_element_type=F32)
        st[c]["yb"] = jnp.dot(hs_ref[rows, :], wb_ref[...], preferred_element_type=F32)

    def merge(c):
        rows = st[c]["rows"]
        y = (_sigmoid(ga_ref[rows, :].astype(F32)) * st[c].pop("ya")
             + _sigmoid(gb_ref[rows, :].astype(F32)) * st[c].pop("yb"))
        st[c]["y"] = y.astype(BF16)

    def project(c):
        st[c]["mix"] = jnp.dot(st[c].pop("y"), wo_ref[...], preferred_element_type=F32)

    def residual(c):
        rows = st[c]["rows"]
        x1 = x_ref[rows, :] + m[2:3] * st[c].pop("mix")
        x1_ref[rows, :] = x1
        ms = jnp.mean(x1 * x1, axis=-1, keepdims=True)
        h2 = x1 * lax.rsqrt(ms + EPS) * nw_ref[...]
        h2 = h2 * (1.0 + m[4:5]) + m[3:4]
        h2_ref[rows, :] = h2.astype(BF16)
        st[c]["h2"] = h2

    def logits(c):
        st[c]["logits"] = _route_logits(st[c].pop("h2"), rwt_ref)

    def select(c):
        gates_ref[st[c]["rows"], :] = _route_select(st[c].pop("logits"), rb_ref)

    stages = [branches, merge, project, residual, logits, select]
    for t in range(len(stages) + n_chunks - 1):
        for c in range(n_chunks):
            if 0 <= t - c < len(stages):
                stages[t - c](c)


def _mix(hm, hs, p, x2d, mod_l, wa, wb, wo, norm2_w, rwt, rb, seq):
    n = x2d.shape[0]
    tiles_per_seq = seq // TM_MIX
    tile = lambda i: (i, 0)
    const = lambda i: (0, 0)
    return pl.pallas_call(
        _mix_kernel,
        grid=(n // TM_MIX,),
        in_specs=[pl.BlockSpec((TM_MIX, D), tile),
                  pl.BlockSpec((TM_MIX, D), tile),
                  pl.BlockSpec((TM_MIX, D), lambda i: (i, P_GA)),
                  pl.BlockSpec((TM_MIX, D), lambda i: (i, P_GB)),
                  pl.BlockSpec((TM_MIX, D), tile),
                  pl.BlockSpec((1, 6, D), lambda i: (i // tiles_per_seq, 0, 0)),
                  pl.BlockSpec((D, D), const, pipeline_mode=pl.Buffered(1)),
                  pl.BlockSpec((D, D), const, pipeline_mode=pl.Buffered(1)),
                  pl.BlockSpec((D, D), const, pipeline_mode=pl.Buffered(1)),
                  pl.BlockSpec((1, D), const),
                  pl.BlockSpec((N_EXPERTS, D), const),
                  pl.BlockSpec((N_EXPERTS, 1), const)],
        out_specs=[pl.BlockSpec((TM_MIX, D), tile),
                   pl.BlockSpec((TM_MIX, D), tile),
                   pl.BlockSpec((TM_MIX, LANES), tile)],
        out_shape=[jax.ShapeDtypeStruct((n, D), F32),
                   jax.ShapeDtypeStruct((n, D), BF16),
                   jax.ShapeDtypeStruct((n, LANES), F32)],
        compiler_params=_params(("parallel",)),
        name="mix_out",
    )(hm, hs, p, p, x2d, mod_l, wa, wb, wo, norm2_w.reshape(1, D), rwt, rb)


def _moe_kernel(h2_ref, gates_ref, x1_ref, mod_ref, wg_ref, wu_ref, wd_ref,
                sg_ref, su_ref, sd_ref, fnw_ref, o_ref, wg_b, wu_b, wd_b, *, final_norm):
    e = pl.program_id(1)
    chunks = [pl.ds(r * R_MOE, R_MOE) for r in range(TM_MOE // R_MOE)]

    def swiglu(rows, gate_cols):
        xr = h2_ref[rows, :]
        g = jnp.dot(xr, wg_b[...], preferred_element_type=F32)
        u = jnp.dot(xr, wu_b[...], preferred_element_type=F32)
        hid = g * _sigmoid(g) * u
        if gate_cols is not None:
            hid = hid * gate_cols
        return jnp.dot(hid.astype(BF16), wd_b[...], preferred_element_type=F32)

    @pl.when(e == 0)
    def _():
        wg_b[...] = sg_ref[0].astype(BF16)
        wu_b[...] = su_ref[0].astype(BF16)
        wd_b[...] = sd_ref[0].astype(BF16)
        for rows in chunks:
            o_ref[rows, :] = swiglu(rows, None)

    wg_b[:, :EXPERT_FF] = wg_ref[0, 0].astype(BF16)
    wg_b[:, EXPERT_FF:] = wg_ref[0, 1].astype(BF16)
    wu_b[:, :EXPERT_FF] = wu_ref[0, 0].astype(BF16)
    wu_b[:, EXPERT_FF:] = wu_ref[0, 1].astype(BF16)
    wd_b[:EXPERT_FF, :] = wd_ref[0, 0].astype(BF16)
    wd_b[EXPERT_FF:, :] = wd_ref[0, 1].astype(BF16)
    lane = lax.broadcasted_iota(jnp.int32, (1, LANES), 1)
    first_half = lax.broadcasted_iota(jnp.int32, (1, 2 * EXPERT_FF), 1) < EXPERT_FF
    for rows in chunks:
        gts = gates_ref[rows, :]
        g0 = jnp.sum(jnp.where(lane == 2 * e, gts, 0.0), axis=1, keepdims=True)
        g1 = jnp.sum(jnp.where(lane == 2 * e + 1, gts, 0.0), axis=1, keepdims=True)
        o_ref[rows, :] += swiglu(rows, jnp.where(first_half, g0, g1))

    @pl.when(e == pl.num_programs(1) - 1)
    def _():
        x2 = x1_ref[...] + mod_ref[0][5:6] * o_ref[...]
        if final_norm:
            ms = jnp.mean(x2 * x2, axis=-1, keepdims=True)
            x2 = x2 * lax.rsqrt(ms + EPS) * fnw_ref[...]
        o_ref[...] = x2


def _moe(h2, gates, x1, mod_l, layer, w_gate, w_up, w_down, sh_gate, sh_up, sh_down, fnw, seq, final_norm):
    n = h2.shape[0]
    tiles_per_seq = seq // TM_MOE
    tile = lambda i, e: (i, 0)
    once = pl.Buffered(1)
    return pl.pallas_call(
        functools.partial(_moe_kernel, final_norm=final_norm),
        grid=(n // TM_MOE, N_EXPERTS // 2),
        in_specs=[pl.BlockSpec((TM_MOE, D), tile, pipeline_mode=once),
                  pl.BlockSpec((TM_MOE, LANES), tile, pipeline_mode=once),
                  pl.BlockSpec((TM_MOE, D), tile, pipeline_mode=once),
                  pl.BlockSpec((1, 6, D), lambda i, e: (i // tiles_per_seq, 0, 0)),
                  pl.BlockSpec((1, 2, D, EXPERT_FF), lambda i, e: (layer, e, 0, 0)),
                  pl.BlockSpec((1, 2, D, EXPERT_FF), lambda i, e: (layer, e, 0, 0)),
                  pl.BlockSpec((1, 2, EXPERT_FF, D), lambda i, e: (layer, e, 0, 0)),
                  pl.BlockSpec((1, D, SHARED_FF), lambda i, e: (layer, 0, 0), pipeline_mode=once),
                  pl.BlockSpec((1, D, SHARED_FF), lambda i, e: (layer, 0, 0), pipeline_mode=once),
                  pl.BlockSpec((1, SHARED_FF, D), lambda i, e: (layer, 0, 0), pipeline_mode=once),
                  pl.BlockSpec((1, D), lambda i, e: (0, 0))],
        out_specs=pl.BlockSpec((TM_MOE, D), tile),
        out_shape=jax.ShapeDtypeStruct((n, D), F32),
        scratch_shapes=[pltpu.VMEM((D, 2 * EXPERT_FF), BF16),
                        pltpu.VMEM((D, 2 * EXPERT_FF), BF16),
                        pltpu.VMEM((2 * EXPERT_FF, D), BF16)],
        compiler_params=_params(("parallel", "arbitrary")),
        name="moe",
    )(h2, gates, x1, mod_l, w_gate, w_up, w_down, sh_gate, sh_up, sh_down, fnw.reshape(1, D))


def kernel(x, c, ada_w, ada_b, norm1_w, w_in, conv_w, conv_b, m_igate_b, m_fgate_b, m_norm_w, w_proj_a, w_proj_b, w_out, norm2_w, router_w, router_b, w_gate, w_up, w_down, sh_gate, sh_up, sh_down, final_norm_w):
    bsz, seq, _ = x.shape
    n = bsz * seq
    depth = ada_w.shape[0]
    assert 2 * N_MHEADS == SUBLANES and SHARED_FF == 2 * EXPERT_FF and seq % TM_IN == 0 and seq % L_CHUNK == 0 and seq % TQ == 0 and seq >= WIN and WIN % KB == 0 and TQ % KB == 0

    c_pad = jnp.zeros((SUBLANES, D), F32).at[:bsz].set(c)
    mod = _ada(c_pad, ada_w, ada_b)[:, :bsz].reshape(depth, bsz, 6, D)

    gate_lo = 2 * N_MHEADS * DK + 2 * N_MHEADS * DV
    gate_hi = gate_lo + 2 * N_MHEADS
    xc = x.reshape(n, D)
    for l in range(depth):
        w_main, wg_col, wg_row = _repack_w_in(w_in, l, gate_lo, gate_hi - gate_lo)
        gate_b = jnp.concatenate([m_igate_b[l], m_fgate_b[l]]).astype(F32)
        brow = jnp.zeros((1, LANES), F32).at[0, :2 * N_MHEADS].set(gate_b)

        p, gcol, grow = _inproj(xc, mod[l], norm1_w[l], w_main, wg_col, wg_row, seq)
        hm = _mlstm(p, gcol, grow, conv_w[l], conv_b[l], brow, gate_b.reshape(2 * N_MHEADS, 1),
                    m_norm_w[l], bsz, seq)
        hs = _sb_attention(p, bsz, seq)
        x1, h2, gates = _mix(hm, hs, p, xc, mod[l], w_proj_a[l].astype(BF16), w_proj_b[l].astype(BF16),
                             w_out[l].astype(BF16), norm2_w[l], router_w[l].T,
                             router_b[l].reshape(N_EXPERTS, 1), seq)
        xc = _moe(h2, gates, x1, mod[l], l, w_gate, w_up, w_down, sh_gate, sh_up, sh_down,
                  final_norm_w, seq, final_norm=(l == depth - 1))
    return xc.reshape(bsz, seq, D)
```

```python
import functools

import jax
import jax.numpy as jnp
from jax import lax
from jax.experimental import pallas as pl
from jax.experimental.pallas import tpu as pltpu

F32 = jnp.float32
BF16 = jnp.bfloat16

D = 1024
DEPTH = 2
N_MHEADS = 4
DK = 128
DV = 256
CONV_K = 4
SB_HEADS = 16
SB_DH = 64
N_EXPERTS = 64
TOP_K = 8
N_GROUPS = 8
TOPK_GROUPS = 4
GROUP_SIZE = N_EXPERTS // N_GROUPS
EXPERT_FF = 128
SHARED_FF = 256
ROUTED_SCALE = 2.5
EPS = 1e-6

LANES = 128
SUBLANES = 8
VMEM_LIMIT = 56 * 1024 * 1024

P_MQK, P_MV, P_MO, P_SQ, P_SK, P_SV, P_GA, P_GB = range(8)

TM_IN = 2048
TN_IN = 1024
L_CHUNK = 256
TQ = 2048
AV_LAG = 4
KB = 128
WIN = 384
SKIP_LOG = -104.0
TM_MIX = 1024
R_MIX = 512
TM_MOE = 2048
R_MOE = 1024
NEG_INF = float("-inf")
LOG2E = 1.4426950408889634


def _sigmoid(x):
    return 0.5 * jnp.tanh(0.5 * x) + 0.5


def _softplus(x):
    return jnp.maximum(x, 0.0) + jnp.log(1.0 + jnp.exp2(jnp.abs(x) * (-LOG2E)))


def _params(sem):
    return pltpu.CompilerParams(dimension_semantics=sem, vmem_limit_bytes=VMEM_LIMIT)


def _ada_kernel(c_ref, w_ref, b_ref, o_ref):
    c = c_ref[...]
    cond = c * _sigmoid(c)
    o_ref[0] = jnp.dot(cond, w_ref[0], preferred_element_type=F32,
                       precision=lax.Precision.HIGHEST) + b_ref[0]


def _ada(c_pad, ada_w, ada_b):
    tn = 1536
    depth = ada_w.shape[0]
    return pl.pallas_call(
        _ada_kernel,
        grid=(depth, 6 * D // tn),
        in_specs=[pl.BlockSpec((SUBLANES, D), lambda l, j: (0, 0)),
                  pl.BlockSpec((1, D, tn), lambda l, j: (l, 0, j)),
                  pl.BlockSpec((1, 1, tn), lambda l, j: (l, 0, j))],
        out_specs=pl.BlockSpec((1, SUBLANES, tn), lambda l, j: (l, 0, j)),
        out_shape=jax.ShapeDtypeStruct((depth, SUBLANES, 6 * D), F32),
        compiler_params=_params(("parallel", "parallel")),
        name="ada_mod",
    )(c_pad, ada_w, ada_b.reshape(depth, 1, 6 * D))


def _repack_kernel(w_ref, main_ref, gcol_ref, grow_ref, *, gate_lo, n_gate):
    w = w_ref[0]
    width = w.shape[1]
    main_ref[:, :gate_lo] = w[:, :gate_lo].astype(BF16)
    main_ref[:, gate_lo:] = w[:, gate_lo + n_gate:width].astype(BF16)
    lane = lax.broadcasted_iota(jnp.int32, (1, LANES), 1)
    g = jnp.where(lane < n_gate, w[:, gate_lo:gate_lo + LANES], 0.0)
    gcol_ref[...] = g.astype(BF16)
    grow_ref[...] = g.T[:SUBLANES, :].astype(BF16)


def _repack_w_in(w_in, layer, gate_lo, n_gate):
    depth, d, width = w_in.shape
    tr = 256
    return pl.pallas_call(
        functools.partial(_repack_kernel, gate_lo=gate_lo, n_gate=n_gate),
        grid=(d // tr,),
        in_specs=[pl.BlockSpec((1, tr, width), lambda i: (layer, i, 0))],
        out_specs=[pl.BlockSpec((tr, width - n_gate), lambda i: (i, 0)),
                   pl.BlockSpec((tr, LANES), lambda i: (i, 0)),
                   pl.BlockSpec((SUBLANES, tr), lambda i: (0, i))],
        out_shape=[jax.ShapeDtypeStruct((d, width - n_gate), BF16),
                   jax.ShapeDtypeStruct((d, LANES), BF16),
                   jax.ShapeDtypeStruct((SUBLANES, d), BF16)],
        compiler_params=_params(("parallel",)),
        name="repack_w_in",
    )(w_in)


def _inproj_kernel(x_ref, mod_ref, nw_ref, w_ref, wgc_ref, wgr_ref,
                   p_ref, gcol_ref, grow_ref, h_scr):
    @pl.when(pl.program_id(1) == 0)
    def _():
        x = x_ref[...]
        ms = jnp.mean(x * x, axis=-1, keepdims=True)
        y = x * lax.rsqrt(ms + EPS) * nw_ref[...]
        m = mod_ref[0]
        hb = (y * (1.0 + m[1:2]) + m[0:1]).astype(BF16)
        h_scr[...] = hb
        gcol_ref[...] = jnp.dot(hb, wgc_ref[...], preferred_element_type=F32)
        grow_ref[...] = lax.dot_general(wgr_ref[...], hb, (((1,), (1,)), ((), ())),
                                        preferred_element_type=F32)

    p_ref[...] = jnp.dot(h_scr[...], w_ref[...], preferred_element_type=F32).astype(BF16)


def _inproj(x2d, mod_l, norm_w, w_main, wg_col, wg_row, seq):
    n = x2d.shape[0]
    tiles_per_seq = seq // TM_IN
    return pl.pallas_call(
        _inproj_kernel,
        grid=(n // TM_IN, w_main.shape[1] // TN_IN),
        in_specs=[pl.BlockSpec((TM_IN, D), lambda i, j: (i, 0)),
                  pl.BlockSpec((1, 6, D), lambda i, j: (i // tiles_per_seq, 0, 0)),
                  pl.BlockSpec((1, D), lambda i, j: (0, 0)),
                  pl.BlockSpec((D, TN_IN), lambda i, j: (0, j)),
                  pl.BlockSpec((D, LANES), lambda i, j: (0, 0)),
                  pl.BlockSpec((SUBLANES, D), lambda i, j: (0, 0))],
        out_specs=[pl.BlockSpec((TM_IN, TN_IN), lambda i, j: (i, j)),
                   pl.BlockSpec((TM_IN, LANES), lambda i, j: (i, 0)),
                   pl.BlockSpec((SUBLANES, TM_IN), lambda i, j: (0, i))],
        out_shape=[jax.ShapeDtypeStruct((n, w_main.shape[1]), BF16),
                   jax.ShapeDtypeStruct((n, LANES), F32),
                   jax.ShapeDtypeStruct((SUBLANES, n), F32)],
        scratch_shapes=[pltpu.VMEM((TM_IN, D), BF16)],
        compiler_params=_params(("parallel", "arbitrary")),
        name="in_proj",
    )(x2d, mod_l, norm_w.reshape(1, D), w_main, wg_col, wg_row)


def _mlstm_kernel(mqk_ref, mv_ref, mo_ref, gcol_ref, grow_ref, cw_ref, cb_ref, shift_ref,
                  brow_ref, bcol_ref, nw_ref, o_ref, tail_scr, c_scr, m_scr):
    L = L_CHUNK

    @pl.when(pl.program_id(1) == 0)
    def _():
        tail_scr[...] = jnp.zeros_like(tail_scr)
        c_scr[...] = jnp.zeros_like(c_scr)
        m_scr[...] = jnp.zeros_like(m_scr)

    cur_b = mqk_ref[...]
    cur = cur_b.astype(F32)
    tail = tail_scr[...]
    row8 = lax.broadcasted_iota(jnp.int32, (SUBLANES, 1), 0)
    conv = cb_ref[...] + cur * cw_ref[CONV_K - 1:CONV_K, :]
    head = jnp.zeros((SUBLANES, D), F32)
    for d in range(1, CONV_K):
        w_d = cw_ref[CONV_K - 1 - d:CONV_K - d, :]
        conv = conv + jnp.dot(shift_ref[d - 1], cur_b, preferred_element_type=F32) * w_d
        head = head + jnp.where(row8 < d, pltpu.roll(tail, d, axis=0), 0.0) * w_d
    conv = jnp.concatenate([conv[:SUBLANES] + head, conv[SUBLANES:]], axis=0)
    tail_scr[...] = cur[L - SUBLANES:, :]
    qk = conv * _sigmoid(conv)

    gcol = gcol_ref[...] + brow_ref[...]
    grow = grow_ref[...] + bcol_ref[...]
    lf_col_all = -_softplus(-gcol)
    lf_row_all = -_softplus(-grow)

    r_idx = lax.broadcasted_iota(jnp.int32, (L, L), 0)
    c_idx = lax.broadcasted_iota(jnp.int32, (L, L), 1)
    causal = c_idx <= r_idx
    one_col = jnp.where(lax.broadcasted_iota(jnp.int32, (L, LANES), 1) == 0, 1.0, 0.0).astype(BF16)

    for h in range(N_MHEADS):
        ig_row = grow[h:h + 1, :]
        lf_row = lf_row_all[N_MHEADS + h:N_MHEADS + h + 1, :]
        ig_col = gcol[:, h:h + 1]
        lf_col = lf_col_all[:, N_MHEADS + h:N_MHEADS + h + 1]
        m_prev = m_scr[h:h + 1, 0:1]

        b_row = jnp.sum(jnp.where(c_idx >= r_idx, lf_col, 0.0), axis=0, keepdims=True)
        b_col = jnp.sum(jnp.where(causal, lf_row, 0.0), axis=1, keepdims=True)
        u_row = ig_row - b_row
        u_col = ig_col - b_col
        cm_col = jnp.maximum(m_prev, jnp.max(jnp.where(causal, u_row, NEG_INF), axis=1, keepdims=True))
        cm_last = jnp.maximum(m_prev, jnp.max(u_row, axis=1, keepdims=True))
        a_tot = jnp.sum(lf_row, axis=1, keepdims=True)

        w_intra = jnp.where(causal, jnp.exp(u_row - cm_col), 0.0)
        w_inter = jnp.exp(m_prev - cm_col)
        clamp = jnp.exp(-(b_col + cm_col))
        wc_col = jnp.exp(u_col - cm_last)
        decay = jnp.exp(m_prev - cm_last)

        q_h = (qk[:, h * DK:(h + 1) * DK] * (DK ** -0.5)).astype(BF16)
        k_f = qk[:, N_MHEADS * DK + h * DK:N_MHEADS * DK + (h + 1) * DK]
        k_h = k_f.astype(BF16)
        v_aug = jnp.concatenate([mv_ref[:, h * DV:(h + 1) * DV], one_col], axis=1)

        s = lax.dot_general(q_h, k_h, (((1,), (1,)), ((), ())), preferred_element_type=F32)
        s = (s * w_intra).astype(BF16)
        cn = c_scr[h]
        nd = jnp.dot(s, v_aug, preferred_element_type=F32)
        nd = nd + w_inter * jnp.dot(q_h, cn.astype(BF16), preferred_element_type=F32)
        num = nd[:, :DV]
        den = nd[:, DV:DV + 1]
        hh = num / jnp.maximum(jnp.abs(den), clamp)

        kw_t = (k_f * wc_col).T.astype(BF16)
        c_scr[h] = decay * cn + jnp.dot(kw_t, v_aug, preferred_element_type=F32)
        m_scr[h:h + 1, :] = jnp.broadcast_to(a_tot + cm_last, (1, LANES))

        hn = hh * lax.rsqrt(jnp.mean(hh * hh, axis=-1, keepdims=True) + EPS)
        hn = hn * nw_ref[:, h * DV:(h + 1) * DV]
        og = _sigmoid(mo_ref[:, h * DV:(h + 1) * DV].astype(F32))
        o_ref[:, h * DV:(h + 1) * DV] = (hn * og).astype(BF16)


def _mlstm(p, gcol, grow, conv_w, conv_b, brow, bcol, m_norm_w, bsz, seq):
    n = p.shape[0]
    nc = seq // L_CHUNK
    row = lambda b, c: b * nc + c
    t_idx = lax.broadcasted_iota(jnp.int32, (CONV_K - 1, L_CHUNK, L_CHUNK), 1)
    s_idx = lax.broadcasted_iota(jnp.int32, (CONV_K - 1, L_CHUNK, L_CHUNK), 2)
    d_idx = lax.broadcasted_iota(jnp.int32, (CONV_K - 1, L_CHUNK, L_CHUNK), 0) + 1
    shifts = jnp.where(t_idx - s_idx == d_idx, 1.0, 0.0).astype(BF16)
    return pl.pallas_call(
        _mlstm_kernel,
        grid=(bsz, nc),
        in_specs=[pl.BlockSpec((L_CHUNK, D), lambda b, c: (row(b, c), P_MQK)),
                  pl.BlockSpec((L_CHUNK, D), lambda b, c: (row(b, c), P_MV)),
                  pl.BlockSpec((L_CHUNK, D), lambda b, c: (row(b, c), P_MO)),
                  pl.BlockSpec((L_CHUNK, LANES), lambda b, c: (row(b, c), 0)),
                  pl.BlockSpec((SUBLANES, L_CHUNK), lambda b, c: (0, row(b, c))),
                  pl.BlockSpec((CONV_K, D), lambda b, c: (0, 0)),
                  pl.BlockSpec((1, D), lambda b, c: (0, 0)),
                  pl.BlockSpec((CONV_K - 1, L_CHUNK, L_CHUNK), lambda b, c: (0, 0, 0)),
                  pl.BlockSpec((1, LANES), lambda b, c: (0, 0)),
                  pl.BlockSpec((SUBLANES, 1), lambda b, c: (0, 0)),
                  pl.BlockSpec((1, D), lambda b, c: (0, 0))],
        out_specs=pl.BlockSpec((L_CHUNK, D), lambda b, c: (row(b, c), 0)),
        out_shape=jax.ShapeDtypeStruct((n, D), BF16),
        scratch_shapes=[pltpu.VMEM((SUBLANES, D), F32),
                        pltpu.VMEM((N_MHEADS, DK, DV + LANES), F32),
                        pltpu.VMEM((SUBLANES, LANES), F32)],
        compiler_params=_params(("parallel", "arbitrary")),
        name="mlstm",
    )(p, p, p, gcol, grow, conv_w, conv_b.reshape(1, D), shifts, brow, bcol, m_norm_w.reshape(1, D))


def _sb_cum(sp_blocks, tri):
    rows = []
    for sp in sp_blocks:
        rows.append(sp.astype(BF16))
    lhs = rows[0] if len(rows) == 1 else jnp.concatenate(rows, axis=0)
    cum = jnp.dot(lhs, tri, preferred_element_type=F32)
    return [cum[r * KB:(r + 1) * KB] for r in range(len(sp_blocks))]


def _nt_dot(a, b):
    return lax.dot_general(a, b, (((1,), (1,)), ((), ())), preferred_element_type=F32)


def _sb_tail(qm2, accs, carries, n_rem, k_ref, v_ref, tri):
    def live(c0, c1):
        return (jnp.max(jnp.maximum(c0, c1)) > SKIP_LOG).astype(jnp.int32)

    def cond(st):
        return jnp.logical_and(st[0] < n_rem, st[1] > 0)

    def body(st):
        j, _, a0, a1, c0, c1 = st
        k0 = pl.multiple_of((n_rem - 1 - j) * KB, KB)
        k_blk = k_ref[pl.ds(k0, KB), :]
        v_blk = v_ref[pl.ds(k0, KB), :]
        z_b = [_nt_dot(qm, k_blk) for qm in qm2]
        cum_b = _sb_cum([_softplus(z) for z in z_b], tri)
        acc_b, car_b = [a0, a1], [c0, c1]
        for hd in range(2):
            att = jnp.exp(z_b[hd] + cum_b[hd][:, :KB] + car_b[hd]).astype(BF16)
            acc_b[hd] = acc_b[hd] + jnp.dot(att, v_blk, preferred_element_type=F32)
            car_b[hd] = car_b[hd] + cum_b[hd][:, KB:]
        return (j + 1, live(car_b[0], car_b[1]), acc_b[0], acc_b[1], car_b[0], car_b[1])

    st = lax.while_loop(cond, body, (jnp.int32(0), live(carries[0], carries[1]),
                                     accs[0], accs[1], carries[0], carries[1]))
    return st[2], st[3]


def _sb_kernel(q_ref, k_ref, v_ref, tri_ref, o_ref):
    i = pl.program_id(2)
    tri = tri_ref[...]
    lane = lax.broadcasted_iota(jnp.int32, (1, LANES), 1)
    q2 = q_ref[...] * jnp.asarray(SB_DH ** -0.5, BF16)
    qms = [jnp.where((lane >= hd * SB_DH) & (lane < (hd + 1) * SB_DH), q2, jnp.zeros_like(q2))
           for hd in range(2)]
    nsub = TQ // KB
    nb = WIN // KB
    diff = (lax.broadcasted_iota(jnp.int32, (KB, WIN), 1)
            - lax.broadcasted_iota(jnp.int32, (KB, WIN), 0))
    below_diag = diff[:, :KB] < 0

    st = [dict() for _ in range(nsub)]

    def stage_scores(s):
        qs = i * TQ + s * KB
        ws = pl.multiple_of(jnp.maximum(qs - (WIN - KB), 0), KB)
        if s * KB >= WIN - KB:
            masks = [None] * (nb - 1) + [below_diag]
        else:
            valid = diff < (qs - ws)
            masks = [valid[:, kb * KB:(kb + 1) * KB] for kb in range(nb)]
        k_win = k_ref[pl.ds(ws, WIN), :]
        zs = [_nt_dot(qms[hd][s * KB:(s + 1) * KB], k_win) for hd in range(2)]
        sps = []
        for z in zs:
            for kb in range(nb):
                sp = _softplus(z[:, kb * KB:(kb + 1) * KB])
                sps.append(sp if masks[kb] is None else jnp.where(masks[kb], sp, 0.0))
        st[s].update(masks=masks, z=zs, cum=_sb_cum(sps, tri), v=v_ref[pl.ds(ws, WIN), :], n_rem=ws // KB)

    def stage_output(s):
        st[s]["acc"], st[s]["carry"] = [], []
        for hd in range(2):
            carry = jnp.zeros((KB, KB), F32)
            att = [None] * nb
            for kb in reversed(range(nb)):
                cum = st[s]["cum"][hd * nb + kb]
                w = jnp.exp(st[s]["z"][hd][:, kb * KB:(kb + 1) * KB] + cum[:, :KB] + carry)
                m = st[s]["masks"][kb]
                att[kb] = (w if m is None else jnp.where(m, w, 0.0)).astype(BF16)
                carry = carry + cum[:, KB:]
            st[s]["acc"].append(jnp.dot(jnp.concatenate(att, axis=1), st[s]["v"], preferred_element_type=F32))
            st[s]["carry"].append(carry)

    for t in range(nsub + AV_LAG):
        if t < nsub:
            stage_scores(t)
        if 0 <= t - AV_LAG < nsub:
            stage_output(t - AV_LAG)
    accs = [a for s in range(nsub) for a in st[s]["acc"]]
    carries = [c for s in range(nsub) for c in st[s]["carry"]]
    n_rems = [st[s]["n_rem"] for s in range(nsub)]

    worst = functools.reduce(jnp.maximum, carries)
    need_tail = jnp.logical_and(jnp.max(worst) > SKIP_LOG, n_rems[-1] > 0)

    def run_tails():
        res = []
        for s in range(nsub):
            qm2 = [qms[hd][s * KB:(s + 1) * KB] for hd in range(2)]
            res.extend(_sb_tail(qm2, accs[2 * s:2 * s + 2], carries[2 * s:2 * s + 2],
                                n_rems[s], k_ref, v_ref, tri))
        return tuple(res)

    accs = lax.cond(need_tail, run_tails, lambda: tuple(accs))
    o_ref[...] = jnp.concatenate(
        [jnp.where(lane < SB_DH, accs[2 * s], accs[2 * s + 1]).astype(BF16) for s in range(nsub)], axis=0)


def _sb_attention(p, bsz, seq):
    n = p.shape[0]
    nq = seq // TQ
    pairs = SB_HEADS * SB_DH // LANES
    per_blk = D // LANES
    rr = lax.broadcasted_iota(jnp.int32, (KB, 2 * KB), 0)
    cc = lax.broadcasted_iota(jnp.int32, (KB, 2 * KB), 1)
    tri = jnp.where((cc >= KB) | (rr >= cc), -1.0, 0.0).astype(BF16)
    return pl.pallas_call(
        _sb_kernel,
        grid=(bsz, pairs, nq),
        in_specs=[pl.BlockSpec((TQ, LANES), lambda b, g, i: (b * nq + i, P_SQ * per_blk + g)),
                  pl.BlockSpec((seq, LANES), lambda b, g, i: (b, P_SK * per_blk + g)),
                  pl.BlockSpec((seq, LANES), lambda b, g, i: (b, P_SV * per_blk + g)),
                  pl.BlockSpec((KB, 2 * KB), lambda b, g, i: (0, 0))],
        out_specs=pl.BlockSpec((TQ, LANES), lambda b, g, i: (b * nq + i, g)),
        out_shape=jax.ShapeDtypeStruct((n, D), BF16),
        compiler_params=_params(("parallel", "parallel", "arbitrary")),
        name="sb_attention",
    )(p, p, p, tri)


def _route_logits(h2, rwt_ref):
    return lax.dot_general(rwt_ref[...], h2, (((1,), (1,)), ((), ())),
                           preferred_element_type=F32, precision=lax.Precision.HIGHEST)


def _route_select(logits, rb_ref):
    tm = logits.shape[1]
    scores = _sigmoid(logits)
    sel = scores + rb_ref[...]
    sub = lax.broadcasted_iota(jnp.int32, (GROUP_SIZE, tm), 0).astype(F32)
    grp_scores = []
    for g in range(N_GROUPS):
        x = sel[g * GROUP_SIZE:(g + 1) * GROUP_SIZE, :]
        m1 = jnp.max(x, axis=0, keepdims=True)
        first = jnp.min(jnp.where(x == m1, sub, float(GROUP_SIZE)), axis=0, keepdims=True)
        m2 = jnp.max(jnp.where(sub == first, NEG_INF, x), axis=0, keepdims=True)
        grp_scores.append(m1 + m2)
    gs = jnp.concatenate(grp_scores, axis=0)
    g_iota = lax.broadcasted_iota(jnp.int32, (N_GROUPS, tm), 0)
    g_rank = jnp.zeros((N_GROUPS, tm), F32)
    for g in range(N_GROUPS):
        row = gs[g:g + 1, :]
        g_rank = g_rank + jnp.where(row > gs, 1.0, jnp.where(row == gs, jnp.where(g_iota > g, 1.0, 0.0), 0.0))
    g_sel = g_rank < TOPK_GROUPS
    masked = jnp.concatenate(
        [jnp.where(g_sel[g:g + 1, :], sel[g * GROUP_SIZE:(g + 1) * GROUP_SIZE, :], NEG_INF)
         for g in range(N_GROUPS)], axis=0)
    e_iota = lax.broadcasted_iota(jnp.int32, (N_EXPERTS, tm), 0).astype(F32)
    chosen = jnp.zeros((N_EXPERTS, tm), F32)
    rest = masked
    for _ in range(TOP_K):
        top = jnp.max(rest, axis=0, keepdims=True)
        first = jnp.min(jnp.where(rest == top, e_iota, float(N_EXPERTS)), axis=0, keepdims=True)
        hit = e_iota == first
        chosen = jnp.where(hit, 1.0, chosen)
        rest = jnp.where(hit, NEG_INF, rest)
    w_sel = jnp.where(chosen > 0.0, scores, 0.0)
    gates_t = w_sel / jnp.sum(w_sel, axis=0, keepdims=True) * ROUTED_SCALE
    gates_t = jnp.concatenate([gates_t, jnp.zeros((LANES - N_EXPERTS, tm), F32)], axis=0)
    return gates_t.T


def _mix_kernel(hm_ref, hs_ref, ga_ref, gb_ref, x_ref, mod_ref, wa_ref, wb_ref, wo_ref,
                nw_ref, rwt_ref, rb_ref, x1_ref, h2_ref, gates_ref):
    m = mod_ref[0]
    n_chunks = TM_MIX // R_MIX
    st = [dict(rows=pl.ds(c * R_MIX, R_MIX)) for c in range(n_chunks)]

    def branches(c):
        rows = st[c]["rows"]
        st[c]["ya"] = jnp.dot(hm_ref[rows, :], wa_ref[...], preferred_element_type=F32)
        st[c]["yb"] = jnp.dot(hs_ref[rows, :], wb_ref[...], preferred_element_type=F32)

    def merge(c):
        rows = st[c]["rows"]
        y = (_sigmoid(ga_ref[rows, :].astype(F32)) * st[c].pop("ya")
             + _sigmoid(gb_ref[rows, :].astype(F32)) * st[c].pop("yb"))
        st[c]["y"] = y.astype(BF16)

    def project(c):
        st[c]["mix"] = jnp.dot(st[c].pop("y"), wo_ref[...], preferred_element_type=F32)

    def residual(c):
        rows = st[c]["rows"]
        x1 = x_ref[rows, :] + m[2:3] * st[c].pop("mix")
        x1_ref[rows, :] = x1
        ms = jnp.mean(x1 * x1, axis=-1, keepdims=True)
        h2 = x1 * lax.rsqrt(ms + EPS) * nw_ref[...]
        h2 = h2 * (1.0 + m[4:5]) + m[3:4]
        h2_ref[rows, :] = h2.astype(BF16)
        st[c]["h2"] = h2

    def logits(c):
        st[c]["logits"] = _route_logits(st[c].pop("h2"), rwt_ref)

    def select(c):
        gates_ref[st[c]["rows"], :] = _route_select(st[c].pop("logits"), rb_ref)

    stages = [branches, merge, project, residual, logits, select]
    for t in range(len(stages) + n_chunks - 1):
        for c in range(n_chunks):
            if 0 <= t - c < len(stages):
                stages[t - c](c)


def _mix(hm, hs, p, x2d, mod_l, wa, wb, wo, norm2_w, rwt, rb, seq):
    n = x2d.shape[0]
    tiles_per_seq = seq // TM_MIX
    tile = lambda i: (i, 0)
    const = lambda i: (0, 0)
    return pl.pallas_call(
        _mix_kernel,
        grid=(n // TM_MIX,),
        in_specs=[pl.BlockSpec((TM_MIX, D), tile),
                  pl.BlockSpec((TM_MIX, D), tile),
                  pl.BlockSpec((TM_MIX, D), lambda i: (i, P_GA)),
                  pl.BlockSpec((TM_MIX, D), lambda i: (i, P_GB)),
                  pl.BlockSpec((TM_MIX, D), tile),
                  pl.BlockSpec((1, 6, D), lambda i: (i // tiles_per_seq, 0, 0)),
                  pl.BlockSpec((D, D), const, pipeline_mode=pl.Buffered(1)),
                  pl.BlockSpec((D, D), const, pipeline_mode=pl.Buffered(1)),
                  pl.BlockSpec((D, D), const, pipeline_mode=pl.Buffered(1)),
                  pl.BlockSpec((1, D), const),
                  pl.BlockSpec((N_EXPERTS, D), const),
                  pl.BlockSpec((N_EXPERTS, 1), const)],
        out_specs=[pl.BlockSpec((TM_MIX, D), tile),
                   pl.BlockSpec((TM_MIX, D), tile),
                   pl.BlockSpec((TM_MIX, LANES), tile)],
        out_shape=[jax.ShapeDtypeStruct((n, D), F32),
                   jax.ShapeDtypeStruct((n, D), BF16),
                   jax.ShapeDtypeStruct((n, LANES), F32)],
        compiler_params=_params(("parallel",)),
        name="mix_out",
    )(hm, hs, p, p, x2d, mod_l, wa, wb, wo, norm2_w.reshape(1, D), rwt, rb)


def _moe_kernel(h2_ref, gates_ref, x1_ref, mod_ref, wg_ref, wu_ref, wd_ref,
                sg_ref, su_ref, sd_ref, fnw_ref, o_ref, wg_b, wu_b, wd_b, *, final_norm):
    e = pl.program_id(1)
    chunks = [pl.ds(r * R_MOE, R_MOE) for r in range(TM_MOE // R_MOE)]

    def swiglu(rows, gate_cols):
        xr = h2_ref[rows, :]
        g = jnp.dot(xr, wg_b[...], preferred_element_type=F32)
        u = jnp.dot(xr, wu_b[...], preferred_element_type=F32)
        hid = g * _sigmoid(g) * u
        if gate_cols is not None:
            hid = hid * gate_cols
        return jnp.dot(hid.astype(BF16), wd_b[...], preferred_element_type=F32)

    @pl.when(e == 0)
    def _():
        wg_b[...] = sg_ref[0].astype(BF16)
        wu_b[...] = su_ref[0].astype(BF16)
        wd_b[...] = sd_ref[0].astype(BF16)
        for rows in chunks:
            o_ref[rows, :] = swiglu(rows, None)

    wg_b[:, :EXPERT_FF] = wg_ref[0, 0].astype(BF16)
    wg_b[:, EXPERT_FF:] = wg_ref[0, 1].astype(BF16)
    wu_b[:, :EXPERT_FF] = wu_ref[0, 0].astype(BF16)
    wu_b[:, EXPERT_FF:] = wu_ref[0, 1].astype(BF16)
    wd_b[:EXPERT_FF, :] = wd_ref[0, 0].astype(BF16)
    wd_b[EXPERT_FF:, :] = wd_ref[0, 1].astype(BF16)
    lane = lax.broadcasted_iota(jnp.int32, (1, LANES), 1)
    first_half = lax.broadcasted_iota(jnp.int32, (1, 2 * EXPERT_FF), 1) < EXPERT_FF
    for rows in chunks:
        gts = gates_ref[rows, :]
        g0 = jnp.sum(jnp.where(lane == 2 * e, gts, 0.0), axis=1, keepdims=True)
        g1 = jnp.sum(jnp.where(lane == 2 * e + 1, gts, 0.0), axis=1, keepdims=True)
        o_ref[rows, :] += swiglu(rows, jnp.where(first_half, g0, g1))

    @pl.when(e == pl.num_programs(1) - 1)
    def _():
        x2 = x1_ref[...] + mod_ref[0][5:6] * o_ref[...]
        if final_norm:
            ms = jnp.mean(x2 * x2, axis=-1, keepdims=True)
            x2 = x2 * lax.rsqrt(ms + EPS) * fnw_ref[...]
        o_ref[...] = x2


def _moe(h2, gates, x1, mod_l, layer, w_gate, w_up, w_down, sh_gate, sh_up, sh_down, fnw, seq, final_norm):
    n = h2.shape[0]
    tiles_per_seq = seq // TM_MOE
    tile = lambda i, e: (i, 0)
    once = pl.Buffered(1)
    return pl.pallas_call(
        functools.partial(_moe_kernel, final_norm=final_norm),
        grid=(n // TM_MOE, N_EXPERTS // 2),
        in_specs=[pl.BlockSpec((TM_MOE, D), tile, pipeline_mode=once),
                  pl.BlockSpec((TM_MOE, LANES), tile, pipeline_mode=once),
                  pl.BlockSpec((TM_MOE, D), tile, pipeline_mode=once),
                  pl.BlockSpec((1, 6, D), lambda i, e: (i // tiles_per_seq, 0, 0)),
                  pl.BlockSpec((1, 2, D, EXPERT_FF), lambda i, e: (layer, e, 0, 0)),
                  pl.BlockSpec((1, 2, D, EXPERT_FF), lambda i, e: (layer, e, 0, 0)),
                  pl.BlockSpec((1, 2, EXPERT_FF, D), lambda i, e: (layer, e, 0, 0)),
                  pl.BlockSpec((1, D, SHARED_FF), lambda i, e: (layer, 0, 0), pipeline_mode=once),
                  pl.BlockSpec((1, D, SHARED_FF), lambda i, e: (layer, 0, 0), pipeline_mode=once),
                  pl.BlockSpec((1, SHARED_FF, D), lambda i, e: (layer, 0, 0), pipeline_mode=once),
                  pl.BlockSpec((1, D), lambda i, e: (0, 0))],
        out_specs=pl.BlockSpec((TM_MOE, D), tile),
        out_shape=jax.ShapeDtypeStruct((n, D), F32),
        scratch_shapes=[pltpu.VMEM((D, 2 * EXPERT_FF), BF16),
                        pltpu.VMEM((D, 2 * EXPERT_FF), BF16),
                        pltpu.VMEM((2 * EXPERT_FF, D), BF16)],
        compiler_params=_params(("parallel", "arbitrary")),
        name="moe",
    )(h2, gates, x1, mod_l, w_gate, w_up, w_down, sh_gate, sh_up, sh_down, fnw.reshape(1, D))


def kernel(x, c, ada_w, ada_b, norm1_w, w_in, conv_w, conv_b, m_igate_b, m_fgate_b, m_norm_w, w_proj_a, w_proj_b, w_out, norm2_w, router_w, router_b, w_gate, w_up, w_down, sh_gate, sh_up, sh_down, final_norm_w):
    bsz, seq, _ = x.shape
    n = bsz * seq
    depth = ada_w.shape[0]
    assert 2 * N_MHEADS == SUBLANES and SHARED_FF == 2 * EXPERT_FF and seq % TM_IN == 0 and seq % L_CHUNK == 0 and seq % TQ == 0 and seq >= WIN and WIN % KB == 0 and TQ % KB == 0

    c_pad = jnp.zeros((SUBLANES, D), F32).at[:bsz].set(c)
    mod = _ada(c_pad, ada_w, ada_b)[:, :bsz].reshape(depth, bsz, 6, D)

    gate_lo = 2 * N_MHEADS * DK + 2 * N_MHEADS * DV
    gate_hi = gate_lo + 2 * N_MHEADS
    xc = x.reshape(n, D)
    for l in range(depth):
        w_main, wg_col, wg_row = _repack_w_in(w_in, l, gate_lo, gate_hi - gate_lo)
        gate_b = jnp.concatenate([m_igate_b[l], m_fgate_b[l]]).astype(F32)
        brow = jnp.zeros((1, LANES), F32).at[0, :2 * N_MHEADS].set(gate_b)

        p, gcol, grow = _inproj(xc, mod[l], norm1_w[l], w_main, wg_col, wg_row, seq)
        hm = _mlstm(p, gcol, grow, conv_w[l], conv_b[l], brow, gate_b.reshape(2 * N_MHEADS, 1),
                    m_norm_w[l], bsz, seq)
        hs = _sb_attention(p, bsz, seq)
        x1, h2, gates = _mix(hm, hs, p, xc, mod[l], w_proj_a[l].astype(BF16), w_proj_b[l].astype(BF16),
                             w_out[l].astype(BF16), norm2_w[l], router_w[l].T,
                             router_b[l].reshape(N_EXPERTS, 1), seq)
        xc = _moe(h2, gates, x1, mod[l], l, w_gate, w_up, w_down, sh_gate, sh_up, sh_down,
                  final_norm_w, seq, final_norm=(l == depth - 1))
    return xc.reshape(bsz, seq, D)
```
